```python
import math
import jax
import jax.numpy as jnp
from jax import lax
import numpy as np


D_MODEL = 1024
BATCH = 4
SEQ = 8192
DEPTH = 4
DEC_BATCH = 2
DEC_SEQ = 16384
PAST_LEN = 128

D_MIX = D_MODEL
SSD_WIDTH = D_MIX // 2
SSD_HEADDIM = 64
SSD_HEADS = SSD_WIDTH // SSD_HEADDIM
SSD_GROUPS = 2
SSD_STATE = 128
SSD_CONV = 5
SSD_CHUNK = 128
SSD_CONV_CH = SSD_WIDTH + 2 * SSD_GROUPS * SSD_STATE
S5_WIDTH = D_MIX - SSD_WIDTH
S5_GROUP_CH = 16
S5_GROUPS = S5_WIDTH // S5_GROUP_CH
S5_STATE = 64
S5_MIN_DECAY = 1e-4
IN_COLS = SSD_WIDTH + SSD_CONV_CH + 2 * SSD_HEADS + S5_WIDTH
N_EXPERTS = 16
N_EXPERT_GROUPS = 4
EXPERTS_PER_GROUP = N_EXPERTS // N_EXPERT_GROUPS
TOP_K = 2
D_EXPERT = D_MODEL // 2
MOE_BLOCK = 128
DEEPNORM_ALPHA = (2 * DEPTH) ** 0.25
DEEPNORM_BETA = (8 * DEPTH) ** -0.25
LN_EPS = 1e-5
RMS_EPS = 1e-5

kernel_name = 'hymba_ssd_s5_deepnorm_moe_encoder'


def _layer_norm(x32, g, b):
    mu = jnp.mean(x32, axis=-1, keepdims=True)
    xc = x32 - mu
    var = jnp.mean(xc * xc, axis=-1, keepdims=True)
    return xc * lax.rsqrt(var + LN_EPS) * g.astype(jnp.float32) + b.astype(jnp.float32)


def _centred_depthwise_conv(x, w, b):
    out = lax.conv_general_dilated(
        x, w[:, None, :], window_strides=(1,),
        padding=[(SSD_CONV // 2, SSD_CONV // 2)],
        dimension_numbers=('NWC', 'WIO', 'NWC'),
        feature_group_count=x.shape[-1])
    return out + b


def _ssd_chunked_scan(x, dt, a, b_mat, c_mat):
    bsz, seqlen = x.shape[0], x.shape[1]
    nc = seqlen // SSD_CHUNK
    r = SSD_HEADS // SSD_GROUPS
    x = x.reshape(bsz, nc, SSD_CHUNK, SSD_GROUPS, r, SSD_HEADDIM)
    dt = dt.reshape(bsz, nc, SSD_CHUNK, SSD_GROUPS, r)
    b_mat = b_mat.reshape(bsz, nc, SSD_CHUNK, SSD_GROUPS, SSD_STATE)
    c_mat = c_mat.reshape(bsz, nc, SSD_CHUNK, SSD_GROUPS, SSD_STATE)
    a_cum = jnp.cumsum(dt * a.reshape(SSD_GROUPS, r), axis=2)
    xdt = x * dt[..., None]
    lower = jnp.tril(jnp.ones((SSD_CHUNK, SSD_CHUNK), bool))[None, None, :, :, None, None]
    seg = a_cum[:, :, :, None] - a_cum[:, :, None, :]
    decay = jnp.exp(jnp.where(lower, seg, -jnp.inf))
    cb = jnp.einsum('bcign,bcjgn->bcijg', c_mat, b_mat)
    y_diag = jnp.einsum('bcijgr,bcjgrp->bcigrp', cb[..., None] * decay, xdt)
    to_end = jnp.exp(a_cum[:, :, -1:] - a_cum)
    states = jnp.einsum('bcjgn,bcjgrp->bcgrpn', b_mat, xdt * to_end[..., None])
    chunk_decay = jnp.exp(a_cum[:, :, -1])

    def carry_fn(h, inp):
        s, d = inp
        return h * d[..., None, None] + s, h

    h0 = jnp.zeros((bsz, SSD_GROUPS, r, SSD_HEADDIM, SSD_STATE), x.dtype)
    _, h_in = lax.scan(carry_fn, h0, (jnp.moveaxis(states, 1, 0), jnp.moveaxis(chunk_decay, 1, 0)))
    h_in = jnp.moveaxis(h_in, 0, 1)
    y_off = jnp.einsum('bcign,bcgrpn->bcigrp', c_mat, h_in) * jnp.exp(a_cum)[..., None]
    return (y_diag + y_off).reshape(bsz, seqlen, SSD_HEADS, SSD_HEADDIM)


def _ssd_mixer(z, xbc, dt_raw, conv_w, conv_b, a_log, dt_bias, d_skip, norm_w):
    f32 = jnp.float32
    bsz, seqlen = z.shape[0], z.shape[1]
    xbc = jax.nn.silu(_centred_depthwise_conv(xbc.astype(f32), conv_w.astype(f32), conv_b.astype(f32)))
    xs, b_mat, c_mat = jnp.split(xbc, [SSD_WIDTH, SSD_WIDTH + SSD_GROUPS * SSD_STATE], axis=-1)
    xs = xs.reshape(bsz, seqlen, SSD_HEADS, SSD_HEADDIM)
    b_mat = b_mat.reshape(bsz, seqlen, SSD_GROUPS, SSD_STATE)
    c_mat = c_mat.reshape(bsz, seqlen, SSD_GROUPS, SSD_STATE)
    dt = jax.nn.softplus(dt_raw.astype(f32).reshape(bsz, seqlen, 2, SSD_HEADS) + dt_bias.astype(f32))
    a = -jnp.exp(a_log.astype(f32))
    rev = lambda t: jnp.flip(t, axis=1)
    y_fwd = _ssd_chunked_scan(xs, dt[:, :, 0], a[0], b_mat, c_mat)
    y_bwd = rev(_ssd_chunked_scan(rev(xs), rev(dt[:, :, 1]), a[1], rev(b_mat), rev(c_mat)))
    y = (y_fwd + y_bwd + xs * d_skip.astype(f32)[:, None]).reshape(bsz, seqlen, SSD_WIDTH)
    y = y * jax.nn.silu(z.astype(f32))
    return y * lax.rsqrt(jnp.mean(y * y, axis=-1, keepdims=True) + RMS_EPS) * norm_w.astype(f32)


def _diag_recurrence(e1, e2):
    a1, b1 = e1
    a2, b2 = e2
    return a2 * a1, a2 * b1 + b2


def _s5_mixer(u, lam_re, lam_im, log_step, b_re, b_im, c_re, c_im, d_skip, w_glu, b_glu):
    f32 = jnp.float32
    bsz, seqlen = u.shape[0], u.shape[1]
    ug = u.astype(f32).reshape(bsz, seqlen, S5_GROUPS, S5_GROUP_CH)
    b_in = lax.complex(b_re.astype(f32), b_im.astype(f32))
    y = ug * d_skip.astype(f32).reshape(S5_GROUPS, S5_GROUP_CH)
    ugc = ug.astype(jnp.complex64)
    for direction, reverse in ((0, False), (1, True)):
        lam = lax.complex(jnp.minimum(lam_re[direction].astype(f32), -S5_MIN_DECAY),
                          lam_im[direction].astype(f32))
        step = jnp.exp(log_step[direction].astype(f32))[:, None]
        lam_bar = jnp.exp(lam * step)
        b_bar = ((lam_bar - 1.0) / lam)[..., None] * b_in
        bu = jnp.einsum('blgi,gpi->blgp', ugc, b_bar)
        _, states = lax.associative_scan(
            _diag_recurrence, (jnp.broadcast_to(lam_bar, bu.shape), bu), axis=1, reverse=reverse)
        c_out = lax.complex(c_re[direction].astype(f32), c_im[direction].astype(f32))
        y = y + jnp.real(jnp.einsum('blgp,gip->blgi', states, c_out))
    y = jax.nn.gelu(y.reshape(bsz, seqlen, S5_WIDTH))
    return y * jax.nn.sigmoid(y @ w_glu.astype(f32) + b_glu.astype(f32))


def _moe(h, w_router, b_router, w_gate, w_up, w_down):
    f32 = jnp.float32
    bsz, seqlen, d = h.shape
    n_tok = bsz * seqlen
    xt = h.reshape(n_tok, d)
    probs = jax.nn.softmax(xt.astype(f32) @ w_router.astype(f32) + b_router.astype(f32), axis=-1)
    grp_score = probs.reshape(n_tok, N_EXPERT_GROUPS, EXPERTS_PER_GROUP).max(axis=-1)
    sel_grp = jnp.argmax(grp_score, axis=-1)
    in_grp = (jnp.arange(N_EXPERTS) // EXPERTS_PER_GROUP)[None, :] == sel_grp[:, None]
    top_v, top_i = lax.top_k(jnp.where(in_grp, probs, -1.0), TOP_K)
    gates = top_v / jnp.sum(top_v, axis=-1, keepdims=True)
    n_slots = n_tok * TOP_K
    flat_e = top_i.reshape(-1)
    flat_tok = jnp.arange(n_slots, dtype=jnp.int32) // TOP_K
    flat_w = gates.reshape(-1)
    order = jnp.argsort(flat_e)
    e_sorted = flat_e[order]
    counts = jnp.bincount(flat_e, length=N_EXPERTS)
    starts = jnp.cumsum(counts) - counts
    padded = (counts + MOE_BLOCK - 1) // MOE_BLOCK * MOE_BLOCK
    pend = jnp.cumsum(padded)
    pstart = pend - padded
    dest = pstart[e_sorted] + (jnp.arange(n_slots) - starts[e_sorted])
    n_blocks = -(-n_slots // MOE_BLOCK) + N_EXPERTS
    n_rows = n_blocks * MOE_BLOCK
    row_tok = jnp.zeros((n_rows,), jnp.int32).at[dest].set(flat_tok[order])
    row_w = jnp.zeros((n_rows,), f32).at[dest].set(flat_w[order])
    blk_e = jnp.minimum(jnp.searchsorted(pend, jnp.arange(n_blocks) * MOE_BLOCK, side='right'),
                        N_EXPERTS - 1)

    def expert_block(args):
        tok, e = args
        xb = jnp.take(xt, tok, axis=0)
        hid = jax.nn.silu(xb @ w_gate[e]) * (xb @ w_up[e])
        return hid @ w_down[e]

    out = lax.map(expert_block, (row_tok.reshape(n_blocks, MOE_BLOCK), blk_e))
    y = jnp.zeros((n_tok, d), f32).at[row_tok].add(out.reshape(n_rows, d).astype(f32) * row_w[:, None])
    return y.reshape(bsz, seqlen, d)


def _trunk(x, c, w_ada, b_ada, w_in, conv_w, conv_b, ssd_a_log, ssd_dt_bias, ssd_d, ssd_norm_w,
           s5_lam_re, s5_lam_im, s5_log_step, s5_b_re, s5_b_im, s5_c_re, s5_c_im, s5_d,
           s5_w_glu, s5_b_glu, w_out, ln1_g, ln1_b, ln2_g, ln2_b, w_router, b_router,
           w_gate, w_up, w_down):
    f32 = jnp.float32
    dtype = x.dtype
    c_act = jax.nn.silu(c.astype(f32))
    for i in range(DEPTH):
        mod = c_act @ w_ada[i].astype(f32) + b_ada[i].astype(f32)
        sh1, sc1, g1, sh2, sc2, g2 = [m[:, None, :] for m in jnp.split(mod, 6, axis=-1)]
        h = (x.astype(f32) * (1.0 + sc1) + sh1).astype(dtype)
        proj = h @ w_in[i]
        z, xbc, dt_raw, u = jnp.split(
            proj, [SSD_WIDTH, SSD_WIDTH + SSD_CONV_CH, SSD_WIDTH + SSD_CONV_CH + 2 * SSD_HEADS], axis=-1)
        y_ssd = _ssd_mixer(z, xbc, dt_raw, conv_w[i], conv_b[i], ssd_a_log[i], ssd_dt_bias[i],
                           ssd_d[i], ssd_norm_w[i])
        y_s5 = _s5_mixer(u, s5_lam_re[i], s5_lam_im[i], s5_log_step[i], s5_b_re[i], s5_b_im[i],
                         s5_c_re[i], s5_c_im[i], s5_d[i], s5_w_glu[i], s5_b_glu[i])
        mix = jnp.concatenate([y_ssd, y_s5], axis=-1).astype(dtype) @ w_out[i]
        x = _layer_norm(DEEPNORM_ALPHA * x.astype(f32) + g1 * mix.astype(f32), ln1_g[i], ln1_b[i]).astype(dtype)
        h = (x.astype(f32) * (1.0 + sc2) + sh2).astype(dtype)
        f = _moe(h, w_router, b_router, w_gate[i], w_up[i], w_down[i])
        x = _layer_norm(DEEPNORM_ALPHA * x.astype(f32) + g2 * f, ln2_g[i], ln2_b[i]).astype(dtype)
    return x


def setup_inputs(seed: int = 0) -> dict:
    key = jax.random.key(seed)
    ks = iter(jax.random.split(key, 48))
    f32 = jnp.float32

    def nrm(shape, scale):
        return jax.random.normal(next(ks), shape, f32) * scale

    def unif(shape, lo, hi):
        return jax.random.uniform(next(ks), shape, f32, lo, hi)

    inv_sqrt_d = D_MODEL ** -0.5
    dt0 = jnp.exp(unif((DEPTH, 2, SSD_HEADS), math.log(1e-3), math.log(1e-1)))
    lam_im0 = jnp.pi * jnp.arange(S5_STATE, dtype=f32)
    return {
        'x_prompt': nrm((BATCH, SEQ, D_MODEL), 1.0),
        'x_sample': nrm((DEC_BATCH, DEC_SEQ, D_MODEL), 1.0),
        'c_prompt': nrm((BATCH, D_MODEL), 1.0),
        'c_sample': nrm((DEC_BATCH, D_MODEL), 1.0),
        'w_ada': nrm((DEPTH, D_MODEL, 6 * D_MODEL), 0.5 * inv_sqrt_d),
        'b_ada': nrm((DEPTH, 6 * D_MODEL), 0.01),
        'w_in': nrm((DEPTH, D_MODEL, IN_COLS), inv_sqrt_d),
        'conv_w': nrm((DEPTH, SSD_CONV, SSD_CONV_CH), SSD_CONV ** -0.5),
        'conv_b': nrm((DEPTH, SSD_CONV_CH), 0.01),
        'ssd_a_log': jnp.log(unif((DEPTH, 2, SSD_HEADS), 1.0, 16.0)),
        'ssd_dt_bias': dt0 + jnp.log(-jnp.expm1(-dt0)),
        'ssd_d': 1.0 + nrm((DEPTH, SSD_HEADS), 0.01),
        'ssd_norm_w': 1.0 + nrm((DEPTH, SSD_WIDTH), 0.01),
        's5_lam_re': -0.5 + nrm((DEPTH, 2, S5_GROUPS, S5_STATE), 0.01),
        's5_lam_im': lam_im0 + nrm((DEPTH, 2, S5_GROUPS, S5_STATE), 0.01),
        's5_log_step': unif((DEPTH, 2, S5_GROUPS), math.log(1e-3), math.log(1e-1)),
        's5_b_re': nrm((DEPTH, S5_GROUPS, S5_STATE, S5_GROUP_CH), (2 * S5_GROUP_CH) ** -0.5),
        's5_b_im': nrm((DEPTH, S5_GROUPS, S5_STATE, S5_GROUP_CH), (2 * S5_GROUP_CH) ** -0.5),
        's5_c_re': nrm((DEPTH, 2, S5_GROUPS, S5_GROUP_CH, S5_STATE), (2 * S5_STATE) ** -0.5),
        's5_c_im': nrm((DEPTH, 2, S5_GROUPS, S5_GROUP_CH, S5_STATE), (2 * S5_STATE) ** -0.5),
        's5_d': nrm((DEPTH, S5_WIDTH), 1.0),
        's5_w_glu': nrm((DEPTH, S5_WIDTH, S5_WIDTH), S5_WIDTH ** -0.5),
        's5_b_glu': nrm((DEPTH, S5_WIDTH), 0.01),
        'w_out': nrm((DEPTH, D_MIX, D_MODEL), D_MIX ** -0.5 * DEEPNORM_BETA),
        'ln1_g': 1.0 + nrm((DEPTH, D_MODEL), 0.01),
        'ln1_b': nrm((DEPTH, D_MODEL), 0.01),
        'ln2_g': 1.0 + nrm((DEPTH, D_MODEL), 0.01),
        'ln2_b': nrm((DEPTH, D_MODEL), 0.01),
        'w_router': nrm((D_MODEL, N_EXPERTS), inv_sqrt_d),
        'b_router': nrm((N_EXPERTS,), 0.01),
        'w_gate': nrm((DEPTH, N_EXPERTS, D_MODEL, D_EXPERT), inv_sqrt_d),
        'w_up': nrm((DEPTH, N_EXPERTS, D_MODEL, D_EXPERT), inv_sqrt_d),
        'w_down': nrm((DEPTH, N_EXPERTS, D_EXPERT, D_MODEL), D_EXPERT ** -0.5 * DEEPNORM_BETA),
    }


def reference(x_prompt, x_sample, c_prompt, c_sample, w_ada, b_ada, w_in, conv_w, conv_b,
              ssd_a_log, ssd_dt_bias, ssd_d, ssd_norm_w, s5_lam_re, s5_lam_im, s5_log_step,
              s5_b_re, s5_b_im, s5_c_re, s5_c_im, s5_d, s5_w_glu, s5_b_glu, w_out,
              ln1_g, ln1_b, ln2_g, ln2_b, w_router, b_router, w_gate, w_up, w_down):
    weights = (w_ada, b_ada, w_in, conv_w, conv_b, ssd_a_log, ssd_dt_bias, ssd_d, ssd_norm_w,
               s5_lam_re, s5_lam_im, s5_log_step, s5_b_re, s5_b_im, s5_c_re, s5_c_im, s5_d,
               s5_w_glu, s5_b_glu, w_out, ln1_g, ln1_b, ln2_g, ln2_b, w_router, b_router,
               w_gate, w_up, w_down)
    y_prompt = _trunk(x_prompt, c_prompt, *weights)
    y_sample = _trunk(x_sample, c_sample, *weights)
    return (y_prompt, y_sample)
```

```python
import functools
import math

import jax
import jax.numpy as jnp
from jax import lax
from jax.experimental import pallas as pl
from jax.experimental.pallas import tpu as pltpu

F32 = jnp.float32
BF16 = jnp.bfloat16

D_MODEL = 1024
DEPTH = 4
SSD_WIDTH = 512
SSD_HEADDIM = 64
SSD_HEADS = 8
SSD_GROUPS = 2
SSD_STATE = 128
SSD_CONV = 5
SSD_CONV_CH = SSD_WIDTH + 2 * SSD_GROUPS * SSD_STATE
S5_WIDTH = 512
S5_GROUP_CH = 16
S5_GROUPS = 32
S5_STATE = 64
S5_MIN_DECAY = 1e-4
N_EXPERTS = 16
EXPERTS_PER_GROUP = 4
D_EXPERT = 512
DEEPNORM_ALPHA = (2 * DEPTH) ** 0.25
LN_EPS = 1e-5
RMS_EPS = 1e-5

SSD_Q = 128
S5_T = 16
S5_PAIRS = S5_GROUPS // 2
PROJ_COLS = 2 * SSD_WIDTH + SSD_CONV_CH
DT_PAD = 128
CONV_HALO = 16
MOE_ROWS = 128
VMEM_LIMIT = 56 * 1024 * 1024


def _dot(a, b):
    return jnp.dot(a, b, preferred_element_type=F32)


def _dot_nt(a, b):
    return lax.dot_general(a, b, (((1,), (1,)), ((), ())), preferred_element_type=F32)


def _dot_tn(a, b):
    return lax.dot_general(a, b, (((0,), (0,)), ((), ())), preferred_element_type=F32)


def _split_bf16(v):
    hi = v.astype(BF16)
    lo = (v - hi.astype(F32)).astype(BF16)
    return hi, lo


def _silu(v):
    return v * jax.nn.sigmoid(v)


def _layer_norm(v, g, b):
    mu = jnp.mean(v, axis=-1, keepdims=True)
    vc = v - mu
    var = jnp.mean(vc * vc, axis=-1, keepdims=True)
    return vc * lax.rsqrt(var + LN_EPS) * g + b


def _params(*sem):
    return pltpu.CompilerParams(dimension_semantics=sem, vmem_limit_bytes=VMEM_LIMIT)


def _ada_kernel(c_ref, w_ref, b_ref, o_ref):
    ca = _silu(c_ref[...])
    c_hi, c_lo = _split_bf16(ca)
    w_hi, w_lo = _split_bf16(w_ref[0])
    acc = _dot(c_hi, w_hi) + _dot(c_lo, w_hi) + _dot(c_hi, w_lo)
    o_ref[0] = acc + b_ref[0]


def _ada_mod(c_all, w_ada, b_ada):
    rows = c_all.shape[0]
    n_out = w_ada.shape[-1]
    bn = 1536
    return pl.pallas_call(
        _ada_kernel,
        grid=(DEPTH, n_out // bn),
        in_specs=[
            pl.BlockSpec((rows, D_MODEL), lambda l, j: (0, 0)),
            pl.BlockSpec((1, D_MODEL, bn), lambda l, j: (l, 0, j)),
            pl.BlockSpec((1, 1, bn), lambda l, j: (l, 0, j)),
        ],
        out_specs=pl.BlockSpec((1, rows, bn), lambda l, j: (l, 0, j)),
        out_shape=jax.ShapeDtypeStruct((DEPTH, rows, n_out), F32),
        compiler_params=_params("arbitrary", "arbitrary"),
    )(c_all, w_ada, b_ada.reshape(DEPTH, 1, n_out))


def _inproj_kernel(x_ref, sc_ref, sh_ref, w_ref, wdt_t_ref,
                   z_ref, xbc_ref, u_ref, dt_ref, dt_t_ref):
    h = (x_ref[0] * (1.0 + sc_ref[0]) + sh_ref[0]).astype(BF16)
    p = _dot(h, w_ref[...])
    z_ref[0] = p[:, :SSD_WIDTH].astype(BF16)
    xbc_ref[0] = p[:, SSD_WIDTH:SSD_WIDTH + SSD_CONV_CH].astype(BF16)
    u_ref[0] = p[:, SSD_WIDTH + SSD_CONV_CH:PROJ_COLS].astype(BF16)
    dt_ref[0] = p[:, PROJ_COLS:PROJ_COLS + 2 * SSD_HEADS]
    dt_t_ref[0] = _dot_nt(wdt_t_ref[...], h)


def _inproj(x, sc, sh, w_all, wdt_t, tm):
    bsz, seqlen, _ = x.shape
    tok = lambda b, i: (b, i, 0)
    per_b = lambda b, i: (b, 0, 0)
    const = lambda b, i: (0, 0)
    return pl.pallas_call(
        _inproj_kernel,
        grid=(bsz, seqlen // tm),
        in_specs=[
            pl.BlockSpec((1, tm, D_MODEL), tok),
            pl.BlockSpec((1, 1, D_MODEL), per_b),
            pl.BlockSpec((1, 1, D_MODEL), per_b),
            pl.BlockSpec(w_all.shape, const),
            pl.BlockSpec(wdt_t.shape, const),
        ],
        out_specs=[
            pl.BlockSpec((1, tm, SSD_WIDTH), tok),
            pl.BlockSpec((1, tm, SSD_CONV_CH), tok),
            pl.BlockSpec((1, tm, S5_WIDTH), tok),
            pl.BlockSpec((1, tm, 2 * SSD_HEADS), tok),
            pl.BlockSpec((1, 2 * SSD_HEADS, tm), lambda b, i: (b, 0, i)),
        ],
        out_shape=[
            jax.ShapeDtypeStruct((bsz, seqlen, SSD_WIDTH), BF16),
            jax.ShapeDtypeStruct((bsz, seqlen, SSD_CONV_CH), BF16),
            jax.ShapeDtypeStruct((bsz, seqlen, S5_WIDTH), BF16),
            jax.ShapeDtypeStruct((bsz, seqlen, 2 * SSD_HEADS), F32),
            jax.ShapeDtypeStruct((bsz, 2 * SSD_HEADS, seqlen), F32),
        ],
        compiler_params=_params("parallel", "parallel"),
    )(x, sc, sh, w_all, wdt_t)


def _conv_kernel(xm_ref, xl_ref, xr_ref, w_ref, b_ref, o_ref, ext_ref):
    i = pl.program_id(1)
    n = pl.num_programs(1)
    tc = xm_ref.shape[1]
    has_left = (i > 0).astype(F32)
    has_right = (i < n - 1).astype(F32)
    ext_ref[0:CONV_HALO, :] = xl_ref[0].astype(F32) * has_left
    ext_ref[CONV_HALO:CONV_HALO + tc, :] = xm_ref[0].astype(F32)
    ext_ref[CONV_HALO + tc:2 * CONV_HALO + tc, :] = xr_ref[0].astype(F32) * has_right
    acc = jnp.zeros((tc, SSD_CONV_CH), F32) + b_ref[...]
    for k in range(SSD_CONV):
        start = CONV_HALO - SSD_CONV // 2 + k
        acc = acc + ext_ref[start:start + tc, :] * w_ref[k:k + 1, :]
    o_ref[0] = _silu(acc).astype(BF16)


def _conv_silu(xbc, conv_w, conv_b, tc):
    bsz, seqlen, ch = xbc.shape
    per_blk = tc // CONV_HALO
    n_halo = seqlen // CONV_HALO
    w_pad = jnp.zeros((8, ch), F32).at[:SSD_CONV].set(conv_w.astype(F32))
    return pl.pallas_call(
        _conv_kernel,
        grid=(bsz, seqlen // tc),
        in_specs=[
            pl.BlockSpec((1, tc, ch), lambda b, i: (b, i, 0)),
            pl.BlockSpec((1, CONV_HALO, ch), lambda b, i: (b, jnp.maximum(i * per_blk - 1, 0), 0)),
            pl.BlockSpec((1, CONV_HALO, ch),
                         lambda b, i: (b, jnp.minimum((i + 1) * per_blk, n_halo - 1), 0)),
            pl.BlockSpec((8, ch), lambda b, i: (0, 0)),
            pl.BlockSpec((1, ch), lambda b, i: (0, 0)),
        ],
        out_specs=pl.BlockSpec((1, tc, ch), lambda b, i: (b, i, 0)),
        out_shape=jax.ShapeDtypeStruct((bsz, seqlen, ch), BF16),
        scratch_shapes=[pltpu.VMEM((tc + 2 * CONV_HALO, ch), F32)],
        compiler_params=_params("parallel", "parallel"),
    )(xbc, xbc, xbc, w_pad, conv_b.astype(F32).reshape(1, ch))


def _softplus(v):
    return jnp.maximum(v, 0.0) + jnp.log1p(jnp.exp(-jnp.abs(v)))


def _expand_heads(cols):
    rows = cols.shape[0]
    lane = lax.broadcasted_iota(jnp.int32, (rows, 128), 1)
    parts = []
    for k in range(SSD_HEADS // 2):
        lo = jnp.broadcast_to(cols[:, 2 * k:2 * k + 1], (rows, 128))
        hi = jnp.broadcast_to(cols[:, 2 * k + 1:2 * k + 2], (rows, 128))
        parts.append(jnp.where(lane < SSD_HEADDIM, lo, hi))
    return jnp.concatenate(parts, axis=1)


def _ssd_chunk(xbc, dt_raw, dt_raw_t, a_row, a_col, bias_row, bias_col, state_ref, rev):
    q = xbc.shape[0]
    x = xbc[:, :SSD_WIDTH]
    b_mat = xbc[:, SSD_WIDTH:SSD_WIDTH + SSD_GROUPS * SSD_STATE]
    c_mat = xbc[:, SSD_WIDTH + SSD_GROUPS * SSD_STATE:]
    dts = _softplus(dt_raw + bias_row)
    dts_t = _softplus(dt_raw_t + bias_col)
    dta = dts * a_row
    dta_t = dts_t * a_col

    row = lax.broadcasted_iota(jnp.int32, (q, q), 0)
    col = lax.broadcasted_iota(jnp.int32, (q, q), 1)
    lower = col <= row
    upper = col >= row
    tri_lower = jnp.where(lower, 1.0, 0.0).astype(BF16)
    tri_upper = jnp.where(upper, 1.0, 0.0).astype(BF16)
    tri_c, tri_r, mask = (tri_upper, tri_lower, upper) if rev else (tri_lower, tri_upper, lower)

    d_hi, d_lo = _split_bf16(dta)
    acc = _dot(tri_c, d_hi) + _dot(tri_c, d_lo)
    t_hi, t_lo = _split_bf16(dta_t)
    acc_t = _dot(t_hi, tri_r) + _dot(t_lo, tri_r)

    lane128 = lax.broadcasted_iota(jnp.int32, (q, 128), 1)
    cb = [_dot_nt(c_mat[:, g * SSD_STATE:(g + 1) * SSD_STATE],
                  b_mat[:, g * SSD_STATE:(g + 1) * SSD_STATE]) for g in range(SSD_GROUPS)]
    ys = []
    for k in range(SSD_HEADS // 2):
        xp = x[:, 128 * k:128 * (k + 1)]
        outs = []
        for hh in range(2):
            h = 2 * k + hh
            g = h // (SSD_HEADS // SSD_GROUPS)
            seg = acc[:, h:h + 1] - acc_t[h:h + 1, :]
            dec = jnp.where(mask, jnp.exp(jnp.minimum(seg, 0.0)), 0.0)
            m_h = (cb[g] * dec * dts_t[h:h + 1, :]).astype(BF16)
            outs.append(_dot(m_h, xp))
        ys.append(jnp.where(lane128 < SSD_HEADDIM, outs[0], outs[1]))
    y = jnp.concatenate(ys, axis=1)

    end = acc[0:1, :] if rev else acc[q - 1:q, :]
    y_scale = _expand_heads(jnp.exp(acc))
    x_w = _expand_heads(dts * jnp.exp(end - acc))
    carry = _expand_heads(jnp.exp(end))
    xw = (x.astype(F32) * x_w).astype(BF16)
    half = SSD_WIDTH // SSD_GROUPS
    y_off = []
    for g in range(SSD_GROUPS):
        st = state_ref[g]
        y_off.append(_dot(c_mat[:, g * SSD_STATE:(g + 1) * SSD_STATE], st.astype(BF16)))
        state_ref[g] = st * carry[:, g * half:(g + 1) * half] + _dot_tn(
            b_mat[:, g * SSD_STATE:(g + 1) * SSD_STATE], xw[:, g * half:(g + 1) * half])
    return y + jnp.concatenate(y_off, axis=1) * y_scale


def _ssd_dir_kernel(*refs, rev):
    if rev:
        (xbc_ref, dt_ref, dt_t_ref, alog_ref, alog_t_ref, bias_ref, bias_t_ref,
         z_ref, yf_ref, dskip_ref, nw_ref, o_ref, state_ref) = refs
    else:
        (xbc_ref, dt_ref, dt_t_ref, alog_ref, alog_t_ref, bias_ref, bias_t_ref,
         o_ref, state_ref) = refs

    @pl.when(pl.program_id(1) == 0)
    def _():
        state_ref[...] = jnp.zeros(state_ref.shape, F32)

    d = 1 if rev else 0
    hs = slice(d * SSD_HEADS, (d + 1) * SSD_HEADS)
    a_row = -jnp.exp(alog_ref[:, hs])
    a_col = -jnp.exp(alog_t_ref[hs, :])
    bias_row = bias_ref[:, hs]
    bias_col = bias_t_ref[hs, :]
    tq = xbc_ref.shape[1]
    n_chunks = tq // SSD_Q
    order = range(n_chunks - 1, -1, -1) if rev else range(n_chunks)
    for ci in order:
        rs = slice(ci * SSD_Q, (ci + 1) * SSD_Q)
        xbc = xbc_ref[0, rs, :]
        y = _ssd_chunk(xbc, dt_ref[0, rs, hs], dt_t_ref[0, hs, rs], a_row, a_col,
                       bias_row, bias_col, state_ref, rev)
        if rev:
            y = y + yf_ref[0, rs, :] + xbc[:, :SSD_WIDTH].astype(F32) * dskip_ref[...]
            y = y * _silu(z_ref[0, rs, :].astype(F32))
            y = y * lax.rsqrt(jnp.mean(y * y, axis=-1, keepdims=True) + RMS_EPS) * nw_ref[...]
            o_ref[0, rs, :] = y.astype(BF16)
        else:
            o_ref[0, rs, :] = y


def _ssd_dir(xbc, dt, dt_t, a_log, dt_bias, rev, tq, z=None, yf=None, d_skip=None, norm_w=None):
    bsz, seqlen, _ = xbc.shape
    nblk = seqlen // tq
    if rev:
        tok = lambda b, i: (b, nblk - 1 - i, 0)
        tok_t = lambda b, i: (b, 0, nblk - 1 - i)
    else:
        tok = lambda b, i: (b, i, 0)
        tok_t = lambda b, i: (b, 0, i)
    const = lambda b, i: (0, 0)
    nh2 = 2 * SSD_HEADS
    a_flat = a_log.astype(F32).reshape(1, nh2)
    bias_flat = dt_bias.astype(F32).reshape(1, nh2)
    args = [xbc, dt, dt_t, a_flat, a_flat.reshape(nh2, 1), bias_flat, bias_flat.reshape(nh2, 1)]
    in_specs = [
        pl.BlockSpec((1, tq, SSD_CONV_CH), tok),
        pl.BlockSpec((1, tq, nh2), tok),
        pl.BlockSpec((1, nh2, tq), tok_t),
        pl.BlockSpec((1, nh2), const),
        pl.BlockSpec((nh2, 1), const),
        pl.BlockSpec((1, nh2), const),
        pl.BlockSpec((nh2, 1), const),
    ]
    if rev:
        args += [z, yf, jnp.repeat(d_skip.astype(F32), SSD_HEADDIM).reshape(1, SSD_WIDTH),
                 norm_w.astype(F32).reshape(1, SSD_WIDTH)]
        in_specs += [
            pl.BlockSpec((1, tq, SSD_WIDTH), tok),
            pl.BlockSpec((1, tq, SSD_WIDTH), tok),
            pl.BlockSpec((1, SSD_WIDTH), const),
            pl.BlockSpec((1, SSD_WIDTH), const),
        ]
    return pl.pallas_call(
        functools.partial(_ssd_dir_kernel, rev=rev),
        grid=(bsz, nblk),
        in_specs=in_specs,
        out_specs=pl.BlockSpec((1, tq, SSD_WIDTH), tok),
        out_shape=jax.ShapeDtypeStruct((bsz, seqlen, SSD_WIDTH), BF16 if rev else F32),
        scratch_shapes=[pltpu.VMEM((SSD_GROUPS, SSD_STATE, SSD_WIDTH // SSD_GROUPS), F32)],
        compiler_params=_params("parallel", "arbitrary"),
    )(*args)


def _s5_tables(lam_re, lam_im, log_step, b_re, b_im, c_re, c_im):
    hp = lax.Precision.HIGHEST
    t = S5_T
    g, p, gc = S5_GROUPS, S5_STATE, S5_GROUP_CH
    lr = jnp.minimum(lam_re.astype(F32), -S5_MIN_DECAY)
    li = lam_im.astype(F32)
    step = jnp.exp(log_step.astype(F32))[..., None]
    ar, ai = lr * step, li * step

    def power(n):
        mag = jnp.exp(n * ar)
        return mag * jnp.cos(n * ai), mag * jnp.sin(n * ai)

    lbr, lbi = power(1.0)
    den = lr * lr + li * li
    qr = ((lbr - 1.0) * lr + lbi * li) / den
    qi = (lbi * lr - (lbr - 1.0) * li) / den
    bre, bim = b_re.astype(F32)[None], b_im.astype(F32)[None]
    bbr = qr[..., None] * bre - qi[..., None] * bim
    bbi = qr[..., None] * bim + qi[..., None] * bre
    cr, ci = c_re.astype(F32), c_im.astype(F32)

    ks = jnp.arange(t + 1, dtype=F32)[:, None, None, None]
    pwr, pwi = power(ks)
    cpr = cr[None] * pwr[:, :, :, None, :] - ci[None] * pwi[:, :, :, None, :]
    cpi = cr[None] * pwi[:, :, :, None, :] + ci[None] * pwr[:, :, :, None, :]
    kern = (jnp.einsum('kdgop,dgpi->kdgoi', cpr, bbr, precision=hp)
            - jnp.einsum('kdgop,dgpi->kdgoi', cpi, bbi, precision=hp))
    tt = jnp.arange(t)
    lag = tt[None, :] - tt[:, None]
    kf = kern[jnp.clip(lag, 0, t), 0]
    kb = kern[jnp.clip(-lag, 0, t), 1]
    m5 = (jnp.where((lag >= 0)[:, :, None, None, None], kf, 0.0)
          + jnp.where((lag <= 0)[:, :, None, None, None], kb, 0.0))
    m = jnp.transpose(m5, (2, 0, 4, 1, 3)).reshape(g, t * gc, t * gc)

    def state_map(pr, pi, d):
        fr = pr[..., None] * bbr[d][None] - pi[..., None] * bbi[d][None]
        fi = pr[..., None] * bbi[d][None] + pi[..., None] * bbr[d][None]
        to_rows = lambda v: jnp.transpose(v, (1, 0, 3, 2)).reshape(g, t * gc, p)
        return to_rows(fr), to_rows(fi)

    f_fr, f_fi = state_map(pwr[:t, 0][::-1], pwi[:t, 0][::-1], 0)
    f_br, f_bi = state_map(pwr[:t, 1], pwi[:t, 1], 1)
    f4 = jnp.stack([f_fr, f_fi, f_br, f_bi], axis=2)
    f4 = f4.reshape(S5_PAIRS, 2, t * gc, 4, p)
    eye2 = jnp.eye(2, dtype=F32)
    f = jnp.einsum('qarsp,ab->qarsbp', f4, eye2).reshape(S5_PAIRS, 2 * t * gc, 4 * 2 * p)

    def out_map(cpr_d, cpi_d):
        to_cols = lambda v: jnp.transpose(v, (1, 3, 0, 2)).reshape(g, p, t * gc)
        return to_cols(cpr_d), to_cols(-cpi_d)

    e_fr, e_fi = out_map(cpr[1:t + 1, 0], cpi[1:t + 1, 0])
    e_br, e_bi = out_map(cpr[1:t + 1, 1][::-1], cpi[1:t + 1, 1][::-1])
    e4 = jnp.stack([e_fr, e_fi, e_br, e_bi], axis=1)
    e4 = e4.reshape(S5_PAIRS, 2, 4, p, t * gc)
    e = jnp.einsum('qaspc,ab->qsapbc', e4, eye2).reshape(S5_PAIRS, 4 * 2 * p, 2 * t * gc)

    def lanes(v):
        lead = v.shape[:-2]
        v = v.reshape(lead + (S5_PAIRS, 2 * p))
        return jnp.moveaxis(v, -2, 0)

    step_r, step_i = power(jnp.array([1.0, 2.0, 4.0, 8.0], F32)[:, None, None, None] * t)
    rows_a = []
    for d in range(2):
        for s in range(4):
            rows_a += [step_r[s, d], step_i[s, d]]
    tab_a = lanes(jnp.stack(rows_a, axis=0))
    r8 = jnp.arange(8, dtype=F32)[:, None, None, None] * t
    car_r, car_i = power(r8)
    tab_b = lanes(jnp.concatenate([car_r[:, 0], car_i[:, 0], car_r[::-1, 1], car_i[::-1, 1]], axis=0))
    tab = jnp.concatenate([tab_a, tab_b], axis=1)
    return m.astype(BF16), f.astype(BF16), e.astype(BF16), tab


def _s5_kernel(u_ref, m_ref, f_ref, e_ref, tab_ref, y_ref, s_ref, xin_ref):
    n_c = u_ref.shape[1]
    lanes = 2 * S5_STATE
    u = u_ref[0]
    s_ref[...] = _dot(u, f_ref[0])

    row = lax.broadcasted_iota(jnp.int32, (8, lanes), 0)
    tab_a = tab_ref[0, 0:16, :]
    bc = lambda v: jnp.broadcast_to(v, (8, lanes))
    step_w = [[(bc(tab_a[8 * d + 2 * s:8 * d + 2 * s + 1]), bc(tab_a[8 * d + 2 * s + 1:8 * d + 2 * s + 2]))
               for s in range(4)] for d in range(2)]
    car = [(tab_ref[0, 16 + 16 * d:24 + 16 * d, :], tab_ref[0, 24 + 16 * d:32 + 16 * d, :])
           for d in range(2)]

    def shifted(v, sh, d):
        if d == 0:
            return jnp.where(row >= sh, pltpu.roll(v, sh, 0), 0.0)
        return jnp.where(row < 8 - sh, pltpu.roll(v, 8 - sh, 0), 0.0)

    def tile_scan(k, hr, hi, d):
        r0 = pl.multiple_of(k * 8, 8)
        c0 = 2 * lanes * d
        zr = s_ref[pl.ds(r0, 8), c0:c0 + lanes]
        zi = s_ref[pl.ds(r0, 8), c0 + lanes:c0 + 2 * lanes]
        for s, sh in enumerate((1, 2, 4)):
            wr, wi = step_w[d][s]
            sr, si = shifted(zr, sh, d), shifted(zi, sh, d)
            zr, zi = zr + wr * sr - wi * si, zi + wr * si + wi * sr
        pr, pi = car[d]
        xin_ref[pl.ds(r0, 8), c0:c0 + lanes] = pr * hr - pi * hi + shifted(zr, 1, d)
        xin_ref[pl.ds(r0, 8), c0 + lanes:c0 + 2 * lanes] = pr * hi + pi * hr + shifted(zi, 1, d)
        last = 0 if d else 7
        wr, wi = step_w[d][3]
        return (wr * hr - wi * hi + bc(zr[last:last + 1]), wr * hi + wi * hr + bc(zi[last:last + 1]))

    n_t = n_c // 8

    def body(k, carry):
        fr, fi, br, bi = carry
        fr, fi = tile_scan(k, fr, fi, 0)
        br, bi = tile_scan(n_t - 1 - k, br, bi, 1)
        return fr, fi, br, bi

    zero = jnp.zeros((8, lanes), F32)
    lax.fori_loop(0, n_t, body, (zero, zero, zero, zero))

    half = S5_T * S5_GROUP_CH
    y_in = _dot(xin_ref[...].astype(BF16), e_ref[0])
    y0 = _dot(u[:, :half], m_ref[0, 0])
    y1 = _dot(u[:, half:], m_ref[0, 1])
    y_ref[0] = jnp.concatenate([y0, y1], axis=1) + y_in


def _s5_scan(u_chunks, m, f, e, tab):
    bsz, n_c, _ = u_chunks.shape
    w = 2 * S5_T * S5_GROUP_CH
    return pl.pallas_call(
        _s5_kernel,
        grid=(bsz, S5_PAIRS),
        in_specs=[
            pl.BlockSpec((1, n_c, w), lambda b, q: (b, 0, q)),
            pl.BlockSpec((1, 2, w // 2, w // 2), lambda b, q: (q, 0, 0, 0)),
            pl.BlockSpec((1, w, w), lambda b, q: (q, 0, 0)),
            pl.BlockSpec((1, w, w), lambda b, q: (q, 0, 0)),
            pl.BlockSpec((1, 48, 2 * S5_STATE), lambda b, q: (q, 0, 0)),
        ],
        out_specs=pl.BlockSpec((1, n_c, w), lambda b, q: (b, 0, q)),
        out_shape=jax.ShapeDtypeStruct((bsz, n_c, S5_GROUPS * S5_T * S5_GROUP_CH), F32),
        scratch_shapes=[pltpu.VMEM((n_c, w), F32), pltpu.VMEM((n_c, w), F32)],
        compiler_params=_params("parallel", "parallel"),
    )(u_chunks, m.reshape(S5_PAIRS, 2, w // 2, w // 2), f, e, tab)


def _to_chunks(u):
    bsz, seqlen, _ = u.shape
    v = u.reshape(bsz, seqlen // S5_T, S5_T, S5_GROUPS, S5_GROUP_CH)
    return jnp.transpose(v, (0, 1, 3, 2, 4)).reshape(bsz, seqlen // S5_T, -1)


def _from_chunks(y, seqlen):
    bsz = y.shape[0]
    v = y.reshape(bsz, seqlen // S5_T, S5_GROUPS, S5_T, S5_GROUP_CH)
    return jnp.transpose(v, (0, 1, 3, 2, 4)).reshape(bsz, seqlen, S5_WIDTH)


def _gelu_tanh(v):
    return 0.5 * v * (1.0 + jnp.tanh(math.sqrt(2.0 / math.pi) * (v + 0.044715 * (v * v * v))))


def _outproj_kernel(yssd_ref, ys5_ref, u_ref, x_ref, g1_ref, sc2_ref, sh2_ref, d5_ref,
                    wglu_ref, bglu_ref, wout_ref, lng_ref, lnb_ref, x1_ref, h2_ref):
    y = ys5_ref[0] + u_ref[0].astype(F32) * d5_ref[...]
    y = _gelu_tanh(y)
    s5 = y * jax.nn.sigmoid(_dot(y.astype(BF16), wglu_ref[...]) + bglu_ref[...])
    mix = (_dot(yssd_ref[0], wout_ref[:SSD_WIDTH, :])
           + _dot(s5.astype(BF16), wout_ref[SSD_WIDTH:, :]))
    x1 = _layer_norm(DEEPNORM_ALPHA * x_ref[0] + g1_ref[0] * mix, lng_ref[...], lnb_ref[...])
    x1_ref[0] = x1
    h2_ref[0] = (x1 * (1.0 + sc2_ref[0]) + sh2_ref[0]).astype(BF16)


def _outproj(yssd, ys5, u, x, g1, sc2, sh2, d5, wglu, bglu, wout, lng, lnb, tm):
    bsz, seqlen, _ = x.shape
    tok = lambda b, i: (b, i, 0)
    per_b = lambda b, i: (b, 0, 0)
    const = lambda b, i: (0, 0)
    row = lambda v: v.astype(F32).reshape(1, -1)
    return pl.pallas_call(
        _outproj_kernel,
        grid=(bsz, seqlen // tm),
        in_specs=[
            pl.BlockSpec((1, tm, SSD_WIDTH), tok),
            pl.BlockSpec((1, tm, S5_WIDTH), tok),
            pl.BlockSpec((1, tm, S5_WIDTH), tok),
            pl.BlockSpec((1, tm, D_MODEL), tok),
            pl.BlockSpec((1, 1, D_MODEL), per_b),
            pl.BlockSpec((1, 1, D_MODEL), per_b),
            pl.BlockSpec((1, 1, D_MODEL), per_b),
            pl.BlockSpec((1, S5_WIDTH), const),
            pl.BlockSpec((S5_WIDTH, S5_WIDTH), const),
            pl.BlockSpec((1, S5_WIDTH), const),
            pl.BlockSpec((D_MODEL, D_MODEL), const),
            pl.BlockSpec((1, D_MODEL), const),
            pl.BlockSpec((1, D_MODEL), const),
        ],
        out_specs=[pl.BlockSpec((1, tm, D_MODEL), tok), pl.BlockSpec((1, tm, D_MODEL), tok)],
        out_shape=[jax.ShapeDtypeStruct((bsz, seqlen, D_MODEL), F32),
                   jax.ShapeDtypeStruct((bsz, seqlen, D_MODEL), BF16)],
        compiler_params=_params("parallel", "parallel"),
    )(yssd, ys5, u, x, g1, sc2, sh2, row(d5), wglu.astype(BF16), row(bglu), wout.astype(BF16),
      row(lng), row(lnb))


def _moe_kernel(h2_ref, x1_ref, g2_ref, wr_t_ref, br_ref, tri_ref, wg_ref, wu_ref, wd_ref,
                lng_ref, lnb_ref, out_ref, sel_ref, pos_ref, gate_ref, acc_ref):
    e = pl.program_id(2)
    tm = h2_ref.shape[1]

    @pl.when(e == 0)
    def _route():
        logits = _dot_nt(wr_t_ref[...], h2_ref[0]) + br_ref[...]
        mx = jnp.max(logits, axis=0, keepdims=True)
        ex = jnp.exp(logits - mx)
        probs = ex / jnp.sum(ex, axis=0, keepdims=True)
        row = lax.broadcasted_iota(jnp.int32, (N_EXPERTS, tm), 0)
        n_grp = N_EXPERTS // EXPERTS_PER_GROUP
        gs = [jnp.max(probs[EXPERTS_PER_GROUP * g:EXPERTS_PER_GROUP * (g + 1)], axis=0, keepdims=True)
              for g in range(n_grp)]
        gmax = functools.reduce(jnp.maximum, gs)
        sel_grp = jnp.full((1, tm), n_grp - 1, jnp.int32)
        for g in range(n_grp - 2, -1, -1):
            sel_grp = jnp.where(gs[g] == gmax, g, sel_grp)
        in_grp = (row // EXPERTS_PER_GROUP) == sel_grp
        masked = jnp.where(in_grp, probs, -1.0)
        m1 = jnp.max(masked, axis=0, keepdims=True)
        i1 = jnp.min(jnp.where(masked == m1, row, N_EXPERTS), axis=0, keepdims=True)
        is1 = row == i1
        masked2 = jnp.where(is1, -2.0, masked)
        m2 = jnp.max(masked2, axis=0, keepdims=True)
        i2 = jnp.min(jnp.where(masked2 == m2, row, N_EXPERTS), axis=0, keepdims=True)
        is2 = row == i2
        den = m1 + m2
        gate_ref[...] = jnp.where(is1, m1 / den, jnp.where(is2, m2 / den, 0.0))
        sel = jnp.where(is1, 1.0, jnp.where(is2, 1.0, 0.0))
        sel_ref[...] = sel
        pos_ref[...] = _dot(sel.astype(BF16), tri_ref[...])
        acc_ref[...] = jnp.zeros(acc_ref.shape, F32)

    sel_row = sel_ref[pl.ds(e, 1), :]
    pos_row = pos_ref[pl.ds(e, 1), :]
    gate_row = gate_ref[pl.ds(e, 1), :]
    count = jnp.sum(sel_row).astype(jnp.int32)
    n_blocks = (count + MOE_ROWS - 1) // MOE_ROWS

    def block(j, carry):
        slot = (lax.broadcasted_iota(jnp.int32, (MOE_ROWS, tm), 0) + j * MOE_ROWS).astype(F32)
        onehot = jnp.where(pos_row == slot, sel_row, 0.0)
        xe = _dot(onehot.astype(BF16), h2_ref[0]).astype(BF16)
        hid = (_silu(_dot(xe, wg_ref[0])) * _dot(xe, wu_ref[0])).astype(BF16)
        out = _dot(hid, wd_ref[0]).astype(BF16)
        acc_ref[...] += _dot_tn((onehot * gate_row).astype(BF16), out)
        return carry

    lax.fori_loop(0, n_blocks, block, 0)

    @pl.when(e == N_EXPERTS - 1)
    def _finish():
        v = DEEPNORM_ALPHA * x1_ref[0] + g2_ref[0] * acc_ref[...]
        out_ref[0] = _layer_norm(v, lng_ref[...], lnb_ref[...])


def _moe(h2, x1, g2, wr_t, br, tri, wg, wu, wd, lng, lnb, tm):
    bsz, seqlen, _ = x1.shape
    tok = lambda b, i, e: (b, i, 0)
    per_b = lambda b, i, e: (b, 0, 0)
    const = lambda b, i, e: (0, 0)
    per_e = lambda b, i, e: (e, 0, 0)
    row = lambda v: v.astype(F32).reshape(1, -1)
    return pl.pallas_call(
        _moe_kernel,
        grid=(bsz, seqlen // tm, N_EXPERTS),
        in_specs=[
            pl.BlockSpec((1, tm, D_MODEL), tok),
            pl.BlockSpec((1, tm, D_MODEL), tok),
            pl.BlockSpec((1, 1, D_MODEL), per_b),
            pl.BlockSpec((N_EXPERTS, D_MODEL), const),
            pl.BlockSpec((N_EXPERTS, 1), const),
            pl.BlockSpec((tm, tm), const),
            pl.BlockSpec((1, D_MODEL, D_EXPERT), per_e),
            pl.BlockSpec((1, D_MODEL, D_EXPERT), per_e),
            pl.BlockSpec((1, D_EXPERT, D_MODEL), per_e),
            pl.BlockSpec((1, D_MODEL), const),
            pl.BlockSpec((1, D_MODEL), const),
        ],
        out_specs=pl.BlockSpec((1, tm, D_MODEL), tok),
        out_shape=jax.ShapeDtypeStruct((bsz, seqlen, D_MODEL), F32),
        scratch_shapes=[pltpu.VMEM((N_EXPERTS, tm), F32), pltpu.VMEM((N_EXPERTS, tm), F32),
                        pltpu.VMEM((N_EXPERTS, tm), F32), pltpu.VMEM((tm, D_MODEL), F32)],
        compiler_params=_params("parallel", "parallel", "arbitrary"),
    )(h2, x1, g2, wr_t, br, tri, wg, wu, wd, row(lng), row(lnb))


def _layer_weights(i, w):
    w_in = w['w_in'][i]
    n_dt = 2 * SSD_HEADS
    dt_lo = SSD_WIDTH + SSD_CONV_CH
    w_dt = w_in[:, dt_lo:dt_lo + n_dt]
    w_all = jnp.concatenate(
        [w_in[:, :dt_lo], w_in[:, dt_lo + n_dt:], w_dt, jnp.zeros((D_MODEL, DT_PAD - n_dt), w_in.dtype)],
        axis=1).astype(BF16)
    m, f, e, tab = _s5_tables(w['s5_lam_re'][i], w['s5_lam_im'][i], w['s5_log_step'][i],
                              w['s5_b_re'][i], w['s5_b_im'][i], w['s5_c_re'][i], w['s5_c_im'][i])
    return dict(
        w_all=w_all, wdt_t=w_dt.T.astype(BF16),
        s5=(m, f, e, tab),
        wg=w['w_gate'][i].astype(BF16), wu=w['w_up'][i].astype(BF16), wd=w['w_down'][i].astype(BF16),
    )


def _trunk(x, mods, w, layer_w, tiles):
    bsz, seqlen, _ = x.shape
    tm, tq, tmoe = tiles
    wr_t = w['w_router'].T.astype(BF16)
    br = w['b_router'].astype(F32).reshape(N_EXPERTS, 1)
    ids = jnp.arange(tmoe)
    tri = (ids[:, None] < ids[None, :]).astype(BF16)
    for i in range(DEPTH):
        lw = layer_w[i]
        sh1, sc1, g1, sh2, sc2, g2 = [v.reshape(bsz, 1, D_MODEL) for v in jnp.split(mods[i], 6, axis=-1)]
        z, xbc, u, dt, dt_t = _inproj(x, sc1, sh1, lw['w_all'], lw['wdt_t'], tm)
        xbc = _conv_silu(xbc, w['conv_w'][i], w['conv_b'][i], tm)
        y_f = _ssd_dir(xbc, dt, dt_t, w['ssd_a_log'][i], w['ssd_dt_bias'][i], False, tq)
        y_ssd = _ssd_dir(xbc, dt, dt_t, w['ssd_a_log'][i], w['ssd_dt_bias'][i], True, tq,
                         z=z, yf=y_f, d_skip=w['ssd_d'][i], norm_w=w['ssd_norm_w'][i])
        y_s5 = _from_chunks(_s5_scan(_to_chunks(u), *lw['s5']), seqlen)
        x1, h2 = _outproj(y_ssd, y_s5, u, x, g1, sc2, sh2, w['s5_d'][i], w['s5_w_glu'][i],
                          w['s5_b_glu'][i], w['w_out'][i], w['ln1_g'][i], w['ln1_b'][i], tm)
        x = _moe(h2, x1, g2, wr_t, br, tri, lw['wg'], lw['wu'], lw['wd'],
                 w['ln2_g'][i], w['ln2_b'][i], tmoe)
    return x


def _pick(n, pref):
    return pref if n % pref == 0 else n


def _run(x_groups, c_groups, w):
    rows = sum(c.shape[0] for c in c_groups)
    pad = (-rows) % 8
    c_all = jnp.concatenate(list(c_groups) + [jnp.zeros((pad, D_MODEL), F32)], axis=0).astype(F32)
    mods = _ada_mod(c_all, w['w_ada'].astype(F32), w['b_ada'].astype(F32))
    layer_w = [_layer_weights(i, w) for i in range(DEPTH)]
    outs = []
    r0 = 0
    for x, c in zip(x_groups, c_groups):
        bsz, seqlen, _ = x.shape
        tiles = (_pick(seqlen, 512), _pick(seqlen, 512), _pick(seqlen, 1024))
        outs.append(_trunk(x, mods[:, r0:r0 + bsz], w, layer_w, tiles))
        r0 += bsz
    return tuple(outs)


def kernel(x_prompt, x_sample, c_prompt, c_sample, w_ada, b_ada, w_in, conv_w, conv_b, ssd_a_log, ssd_dt_bias, ssd_d, ssd_norm_w, s5_lam_re, s5_lam_im, s5_log_step, s5_b_re, s5_b_im, s5_c_re, s5_c_im, s5_d, s5_w_glu, s5_b_glu, w_out, ln1_g, ln1_b, ln2_g, ln2_b, w_router, b_router, w_gate, w_up, w_down):
    w = dict(w_ada=w_ada, b_ada=b_ada, w_in=w_in, conv_w=conv_w, conv_b=conv_b, ssd_a_log=ssd_a_log,
             ssd_dt_bias=ssd_dt_bias, ssd_d=ssd_d, ssd_norm_w=ssd_norm_w, s5_lam_re=s5_lam_re,
             s5_lam_im=s5_lam_im, s5_log_step=s5_log_step, s5_b_re=s5_b_re, s5_b_im=s5_b_im,
             s5_c_re=s5_c_re, s5_c_im=s5_c_im, s5_d=s5_d, s5_w_glu=s5_w_glu, s5_b_glu=s5_b_glu,
             w_out=w_out, ln1_g=ln1_g, ln1_b=ln1_b, ln2_g=ln2_g, ln2_b=ln2_b, w_router=w_router,
             b_router=b_router, w_gate=w_gate, w_up=w_up, w_down=w_down)
    return _run((x_prompt, x_sample), (c_prompt, c_sample), w)
```

```python
import functools
import math

import jax
import jax.numpy as jnp
from jax import lax
from jax.experimental import pallas as pl
from jax.experimental.pallas import tpu as pltpu

F32 = jnp.float32
BF16 = jnp.bfloat16

D_MODEL = 1024
DEPTH = 4
SSD_WIDTH = 512
SSD_HEADDIM = 64
SSD_HEADS = 8
SSD_GROUPS = 2
SSD_STATE = 128
SSD_CONV = 5
SSD_CONV_CH = SSD_WIDTH + 2 * SSD_GROUPS * SSD_STATE
S5_WIDTH = 512
S5_GROUP_CH = 16
S5_GROUPS = 32
S5_STATE = 64
S5_MIN_DECAY = 1e-4
N_EXPERTS = 16
EXPERTS_PER_GROUP = 4
D_EXPERT = 512
DEEPNORM_ALPHA = (2 * DEPTH) ** 0.25
LN_EPS = 1e-5
RMS_EPS = 1e-5

SSD_Q = 128
S5_T = 128
PROJ_COLS = SSD_WIDTH + SSD_CONV_CH
DT_PAD = 128
CONV_HALO = 16
MOE_ROWS = 128
MOE_ALIGN = 16
MOE_GATHER_ROWS = 256
MOE_SUB = 512
VMEM_LIMIT = 56 * 1024 * 1024


def _dot(a, b):
    return jnp.dot(a, b, preferred_element_type=F32)


def _dot_nt(a, b):
    return lax.dot_general(a, b, (((1,), (1,)), ((), ())), preferred_element_type=F32)


def _dot_tn(a, b):
    return lax.dot_general(a, b, (((0,), (0,)), ((), ())), preferred_element_type=F32)


def _split_bf16(v):
    hi = v.astype(BF16)
    lo = (v - hi.astype(F32)).astype(BF16)
    return hi, lo


def _silu(v):
    return v * jax.nn.sigmoid(v)


def _layer_norm(v, g, b):
    mu = jnp.mean(v, axis=-1, keepdims=True)
    vc = v - mu
    var = jnp.mean(vc * vc, axis=-1, keepdims=True)
    return vc * lax.rsqrt(var + LN_EPS) * g + b


def _params(*sem):
    return pltpu.CompilerParams(dimension_semantics=sem, vmem_limit_bytes=VMEM_LIMIT)


def _ada_kernel(c_ref, w_ref, b_ref, o_ref):
    ca = _silu(c_ref[...])
    c_hi, c_lo = _split_bf16(ca)
    w_hi, w_lo = _split_bf16(w_ref[0])
    acc = _dot(c_hi, w_hi) + _dot(c_lo, w_hi) + _dot(c_hi, w_lo)
    o_ref[0] = acc + b_ref[0]


def _ada_mod(c_all, w_ada, b_ada):
    rows = c_all.shape[0]
    n_out = w_ada.shape[-1]
    bn = 1536
    return pl.pallas_call(
        _ada_kernel,
        grid=(DEPTH, n_out // bn),
        in_specs=[
            pl.BlockSpec((rows, D_MODEL), lambda l, j: (0, 0)),
            pl.BlockSpec((1, D_MODEL, bn), lambda l, j: (l, 0, j)),
            pl.BlockSpec((1, 1, bn), lambda l, j: (l, 0, j)),
        ],
        out_specs=pl.BlockSpec((1, rows, bn), lambda l, j: (l, 0, j)),
        out_shape=jax.ShapeDtypeStruct((DEPTH, rows, n_out), F32),
        compiler_params=_params("arbitrary", "arbitrary"),
    )(c_all, w_ada, b_ada.reshape(DEPTH, 1, n_out))


def _inproj_kernel(x_ref, sc_ref, sh_ref, w_ref, w_t_ref,
                   z_ref, xbc_ref, dt_ref, u_t_ref, dt_t_ref):
    h = (x_ref[0] * (1.0 + sc_ref[0]) + sh_ref[0]).astype(BF16)
    p = _dot(h, w_ref[...])
    z_ref[0] = p[:, :SSD_WIDTH].astype(BF16)
    xbc_ref[0] = p[:, SSD_WIDTH:PROJ_COLS].astype(BF16)
    dt_ref[0] = p[:, PROJ_COLS:PROJ_COLS + 2 * SSD_HEADS]
    p_t = _dot_nt(w_t_ref[...], h)
    u_t_ref[0] = p_t[:S5_WIDTH].astype(BF16)
    dt_t_ref[0] = p_t[S5_WIDTH:]


def _inproj(x, sc, sh, w_all, w_t, tm):
    bsz, seqlen, _ = x.shape
    tok = lambda b, i: (b, i, 0)
    tok_t = lambda b, i: (b, 0, i)
    per_b = lambda b, i: (b, 0, 0)
    const = lambda b, i: (0, 0)
    return pl.pallas_call(
        _inproj_kernel,
        grid=(bsz, seqlen // tm),
        in_specs=[
            pl.BlockSpec((1, tm, D_MODEL), tok),
            pl.BlockSpec((1, 1, D_MODEL), per_b),
            pl.BlockSpec((1, 1, D_MODEL), per_b),
            pl.BlockSpec(w_all.shape, const),
            pl.BlockSpec(w_t.shape, const),
        ],
        out_specs=[
            pl.BlockSpec((1, tm, SSD_WIDTH), tok),
            pl.BlockSpec((1, tm, SSD_CONV_CH), tok),
            pl.BlockSpec((1, tm, 2 * SSD_HEADS), tok),
            pl.BlockSpec((1, S5_WIDTH, tm), tok_t),
            pl.BlockSpec((1, 2 * SSD_HEADS, tm), tok_t),
        ],
        out_shape=[
            jax.ShapeDtypeStruct((bsz, seqlen, SSD_WIDTH), BF16),
            jax.ShapeDtypeStruct((bsz, seqlen, SSD_CONV_CH), BF16),
            jax.ShapeDtypeStruct((bsz, seqlen, 2 * SSD_HEADS), F32),
            jax.ShapeDtypeStruct((bsz, S5_WIDTH, seqlen), BF16),
            jax.ShapeDtypeStruct((bsz, 2 * SSD_HEADS, seqlen), F32),
        ],
        compiler_params=_params("parallel", "parallel"),
    )(x, sc, sh, w_all, w_t)


def _conv_kernel(xm_ref, xl_ref, xr_ref, w_ref, b_ref, o_ref, ext_ref):
    i = pl.program_id(1)
    n = pl.num_programs(1)
    tc = xm_ref.shape[1]
    has_left = (i > 0).astype(F32)
    has_right = (i < n - 1).astype(F32)
    ext_ref[0:CONV_HALO, :] = xl_ref[0].astype(F32) * has_left
    ext_ref[CONV_HALO:CONV_HALO + tc, :] = xm_ref[0].astype(F32)
    ext_ref[CONV_HALO + tc:2 * CONV_HALO + tc, :] = xr_ref[0].astype(F32) * has_right
    acc = jnp.zeros((tc, SSD_CONV_CH), F32) + b_ref[...]
    for k in range(SSD_CONV):
        start = CONV_HALO - SSD_CONV // 2 + k
        acc = acc + ext_ref[start:start + tc, :] * w_ref[k:k + 1, :]
    o_ref[0] = _silu(acc).astype(BF16)


def _conv_silu(xbc, conv_w, conv_b, tc):
    bsz, seqlen, ch = xbc.shape
    per_blk = tc // CONV_HALO
    n_halo = seqlen // CONV_HALO
    w_pad = jnp.zeros((8, ch), F32).at[:SSD_CONV].set(conv_w.astype(F32))
    return pl.pallas_call(
        _conv_kernel,
        grid=(bsz, seqlen // tc),
        in_specs=[
            pl.BlockSpec((1, tc, ch), lambda b, i: (b, i, 0)),
            pl.BlockSpec((1, CONV_HALO, ch), lambda b, i: (b, jnp.maximum(i * per_blk - 1, 0), 0)),
            pl.BlockSpec((1, CONV_HALO, ch),
                         lambda b, i: (b, jnp.minimum((i + 1) * per_blk, n_halo - 1), 0)),
            pl.BlockSpec((8, ch), lambda b, i: (0, 0)),
            pl.BlockSpec((1, ch), lambda b, i: (0, 0)),
        ],
        out_specs=pl.BlockSpec((1, tc, ch), lambda b, i: (b, i, 0)),
        out_shape=jax.ShapeDtypeStruct((bsz, seqlen, ch), BF16),
        scratch_shapes=[pltpu.VMEM((tc + 2 * CONV_HALO, ch), F32)],
        compiler_params=_params("parallel", "parallel"),
    )(xbc, xbc, xbc, w_pad, conv_b.astype(F32).reshape(1, ch))


def _softplus(v):
    return jnp.maximum(v, 0.0) + jnp.log1p(jnp.exp(-jnp.abs(v)))


def _expand_heads(cols):
    rows = cols.shape[0]
    lane = lax.broadcasted_iota(jnp.int32, (rows, 128), 1)
    parts = []
    for k in range(SSD_HEADS // 2):
        lo = jnp.broadcast_to(cols[:, 2 * k:2 * k + 1], (rows, 128))
        hi = jnp.broadcast_to(cols[:, 2 * k + 1:2 * k + 2], (rows, 128))
        parts.append(jnp.where(lane < SSD_HEADDIM, lo, hi))
    return jnp.concatenate(parts, axis=1)


def _ssd_chunk(xbc, dt_raw, dt_raw_t, a_row, a_col, bias_row, bias_col, state_ref, rev):
    q = xbc.shape[0]
    x = xbc[:, :SSD_WIDTH]
    b_mat = xbc[:, SSD_WIDTH:SSD_WIDTH + SSD_GROUPS * SSD_STATE]
    c_mat = xbc[:, SSD_WIDTH + SSD_GROUPS * SSD_STATE:]
    dts = _softplus(dt_raw + bias_row)
    dts_t = _softplus(dt_raw_t + bias_col)
    dta = dts * a_row
    dta_t = dts_t * a_col

    row = lax.broadcasted_iota(jnp.int32, (q, q), 0)
    col = lax.broadcasted_iota(jnp.int32, (q, q), 1)
    lower = col <= row
    upper = col >= row
    tri_lower = jnp.where(lower, 1.0, 0.0).astype(BF16)
    tri_upper = jnp.where(upper, 1.0, 0.0).astype(BF16)
    tri_c, tri_r, mask = (tri_upper, tri_lower, upper) if rev else (tri_lower, tri_upper, lower)

    d_hi, d_lo = _split_bf16(dta)
    acc = _dot(tri_c, d_hi) + _dot(tri_c, d_lo)
    t_hi, t_lo = _split_bf16(dta_t)
    acc_t = _dot(t_hi, tri_r) + _dot(t_lo, tri_r)

    lane128 = lax.broadcasted_iota(jnp.int32, (q, 128), 1)
    cb = [_dot_nt(c_mat[:, g * SSD_STATE:(g + 1) * SSD_STATE],
                  b_mat[:, g * SSD_STATE:(g + 1) * SSD_STATE]) for g in range(SSD_GROUPS)]
    ys = []
    for k in range(SSD_HEADS // 2):
        xp = x[:, 128 * k:128 * (k + 1)]
        outs = []
        for hh in range(2):
            h = 2 * k + hh
            g = h // (SSD_HEADS // SSD_GROUPS)
            seg = acc[:, h:h + 1] - acc_t[h:h + 1, :]
            dec = jnp.where(mask, jnp.exp(jnp.minimum(seg, 0.0)), 0.0)
            m_h = (cb[g] * dec * dts_t[h:h + 1, :]).astype(BF16)
            outs.append(_dot(m_h, xp))
        ys.append(jnp.where(lane128 < SSD_HEADDIM, outs[0], outs[1]))
    y = jnp.concatenate(ys, axis=1)

    end = acc[0:1, :] if rev else acc[q - 1:q, :]
    y_scale = _expand_heads(jnp.exp(acc))
    x_w = _expand_heads(dts * jnp.exp(end - acc))
    carry = _expand_heads(jnp.exp(end))
    xw = (x.astype(F32) * x_w).astype(BF16)
    half = SSD_WIDTH // SSD_GROUPS
    y_off = []
    for g in range(SSD_GROUPS):
        st = state_ref[g]
        y_off.append(_dot(c_mat[:, g * SSD_STATE:(g + 1) * SSD_STATE], st.astype(BF16)))
        state_ref[g] = st * carry[:, g * half:(g + 1) * half] + _dot_tn(
            b_mat[:, g * SSD_STATE:(g + 1) * SSD_STATE], xw[:, g * half:(g + 1) * half])
    return y + jnp.concatenate(y_off, axis=1) * y_scale


def _ssd_dir_kernel(*refs, rev):
    if rev:
        (xbc_ref, dt_ref, dt_t_ref, alog_ref, alog_t_ref, bias_ref, bias_t_ref,
         z_ref, yf_ref, dskip_ref, nw_ref, o_ref, state_ref) = refs
    else:
        (xbc_ref, dt_ref, dt_t_ref, alog_ref, alog_t_ref, bias_ref, bias_t_ref,
         o_ref, state_ref) = refs

    @pl.when(pl.program_id(1) == 0)
    def _():
        state_ref[...] = jnp.zeros(state_ref.shape, F32)

    d = 1 if rev else 0
    hs = slice(d * SSD_HEADS, (d + 1) * SSD_HEADS)
    a_row = -jnp.exp(alog_ref[:, hs])
    a_col = -jnp.exp(alog_t_ref[hs, :])
    bias_row = bias_ref[:, hs]
    bias_col = bias_t_ref[hs, :]
    tq = xbc_ref.shape[1]
    n_chunks = tq // SSD_Q
    order = range(n_chunks - 1, -1, -1) if rev else range(n_chunks)
    for ci in order:
        rs = slice(ci * SSD_Q, (ci + 1) * SSD_Q)
        xbc = xbc_ref[0, rs, :]
        y = _ssd_chunk(xbc, dt_ref[0, rs, hs], dt_t_ref[0, hs, rs], a_row, a_col,
                       bias_row, bias_col, state_ref, rev)
        if rev:
            y = y + yf_ref[0, rs, :] + xbc[:, :SSD_WIDTH].astype(F32) * dskip_ref[...]
            y = y * _silu(z_ref[0, rs, :].astype(F32))
            y = y * lax.rsqrt(jnp.mean(y * y, axis=-1, keepdims=True) + RMS_EPS) * nw_ref[...]
            o_ref[0, rs, :] = y.astype(BF16)
        else:
            o_ref[0, rs, :] = y


def _ssd_dir(xbc, dt, dt_t, a_log, dt_bias, rev, tq, z=None, yf=None, d_skip=None, norm_w=None):
    bsz, seqlen, _ = xbc.shape
    nblk = seqlen // tq
    if rev:
        tok = lambda b, i: (b, nblk - 1 - i, 0)
        tok_t = lambda b, i: (b, 0, nblk - 1 - i)
    else:
        tok = lambda b, i: (b, i, 0)
        tok_t = lambda b, i: (b, 0, i)
    const = lambda b, i: (0, 0)
    nh2 = 2 * SSD_HEADS
    a_flat = a_log.astype(F32).reshape(1, nh2)
    bias_flat = dt_bias.astype(F32).reshape(1, nh2)
    args = [xbc, dt, dt_t, a_flat, a_flat.reshape(nh2, 1), bias_flat, bias_flat.reshape(nh2, 1)]
    in_specs = [
        pl.BlockSpec((1, tq, SSD_CONV_CH), tok),
        pl.BlockSpec((1, tq, nh2), tok),
        pl.BlockSpec((1, nh2, tq), tok_t),
        pl.BlockSpec((1, nh2), const),
        pl.BlockSpec((nh2, 1), const),
        pl.BlockSpec((1, nh2), const),
        pl.BlockSpec((nh2, 1), const),
    ]
    if rev:
        args += [z, yf, jnp.repeat(d_skip.astype(F32), SSD_HEADDIM).reshape(1, SSD_WIDTH),
                 norm_w.astype(F32).reshape(1, SSD_WIDTH)]
        in_specs += [
            pl.BlockSpec((1, tq, SSD_WIDTH), tok),
            pl.BlockSpec((1, tq, SSD_WIDTH), tok),
            pl.BlockSpec((1, SSD_WIDTH), const),
            pl.BlockSpec((1, SSD_WIDTH), const),
        ]
    return pl.pallas_call(
        functools.partial(_ssd_dir_kernel, rev=rev),
        grid=(bsz, nblk),
        in_specs=in_specs,
        out_specs=pl.BlockSpec((1, tq, SSD_WIDTH), tok),
        out_shape=jax.ShapeDtypeStruct((bsz, seqlen, SSD_WIDTH), BF16 if rev else F32),
        scratch_shapes=[pltpu.VMEM((SSD_GROUPS, SSD_STATE, SSD_WIDTH // SSD_GROUPS), F32)],
        compiler_params=_params("parallel", "arbitrary"),
    )(*args)


def _s5_tables(lam_re, lam_im, log_step, b_re, b_im, c_re, c_im):
    hp = lax.Precision.HIGHEST
    t = S5_T
    g, p, gc = S5_GROUPS, S5_STATE, S5_GROUP_CH
    lr = jnp.minimum(lam_re.astype(F32), -S5_MIN_DECAY)
    li = lam_im.astype(F32)
    step = jnp.exp(log_step.astype(F32))[..., None]
    ar, ai = lr * step, li * step

    def power(n):
        mag = jnp.exp(n * ar)
        return mag * jnp.cos(n * ai), mag * jnp.sin(n * ai)

    lbr, lbi = power(1.0)
    den = lr * lr + li * li
    qr = ((lbr - 1.0) * lr + lbi * li) / den
    qi = (lbi * lr - (lbr - 1.0) * li) / den
    bre, bim = b_re.astype(F32)[None], b_im.astype(F32)[None]
    bbr = qr[..., None] * bre - qi[..., None] * bim
    bbi = qr[..., None] * bim + qi[..., None] * bre
    cr, ci = c_re.astype(F32), c_im.astype(F32)

    ks = jnp.arange(t + 1, dtype=F32)[:, None, None, None]
    pwr, pwi = power(ks)
    cpr = cr[None] * pwr[:, :, :, None, :] - ci[None] * pwi[:, :, :, None, :]
    cpi = cr[None] * pwi[:, :, :, None, :] + ci[None] * pwr[:, :, :, None, :]
    kern = (jnp.einsum('kdgop,dgpi->kdgoi', cpr, bbr, precision=hp)
            - jnp.einsum('kdgop,dgpi->kdgoi', cpi, bbi, precision=hp))
    kf, kb = kern[:t, 0], kern[:t, 1]
    kall = jnp.concatenate([kb[:0:-1], (kf[0] + kb[0])[None], kf[1:]], axis=0)
    kall = jnp.transpose(kall, (1, 3, 2, 0)).astype(BF16)
    m = jnp.stack([kall[..., t - 1 - tt:2 * t - 1 - tt] for tt in range(t)], axis=2)
    m = m.reshape(g, gc * t, gc * t)

    pad = lambda v, axis: jnp.concatenate([v, jnp.zeros_like(v)], axis=axis)

    def state_map(pr, pi, d):
        fr = pr[..., None] * bbr[d][None] - pi[..., None] * bbi[d][None]
        fi = pr[..., None] * bbi[d][None] + pi[..., None] * bbr[d][None]
        to_rows = lambda v: pad(jnp.transpose(v, (1, 3, 0, 2)).reshape(g, gc * t, p), 2)
        return to_rows(fr), to_rows(fi)

    f_fr, f_fi = state_map(pwr[:t, 0][::-1], pwi[:t, 0][::-1], 0)
    f_br, f_bi = state_map(pwr[:t, 1], pwi[:t, 1], 1)
    f = jnp.concatenate([f_fr, f_fi, f_br, f_bi], axis=2)

    def out_map(cpr_d, cpi_d):
        to_cols = lambda v: pad(jnp.transpose(v, (1, 3, 2, 0)).reshape(g, p, gc * t), 1)
        return to_cols(cpr_d), to_cols(-cpi_d)

    e_fr, e_fi = out_map(cpr[1:t + 1, 0], cpi[1:t + 1, 0])
    e_br, e_bi = out_map(cpr[1:t + 1, 1][::-1], cpi[1:t + 1, 1][::-1])
    e = jnp.concatenate([e_fr, e_fi, e_br, e_bi], axis=1)

    step_r, step_i = power(jnp.array([1.0, 2.0, 4.0, 8.0], F32)[:, None, None, None] * t)
    rows_a = []
    for d in range(2):
        for s in range(4):
            rows_a += [step_r[s, d], step_i[s, d]]
    r8 = jnp.arange(8, dtype=F32)[:, None, None, None] * t
    car_r, car_i = power(r8)
    rows = jnp.concatenate([jnp.stack(rows_a, axis=0), car_r[:, 0], car_i[:, 0],
                            car_r[::-1, 1], car_i[::-1, 1]], axis=0)
    tab = pad(jnp.transpose(rows, (1, 0, 2)), 2)
    return m, f.astype(BF16), e.astype(BF16), tab


def _s5_kernel(u_ref, m_ref, f_ref, e_ref, tab_ref, y_ref, s_ref, xin_ref):
    bsz, gc, n_c, t = u_ref.shape
    lanes = 2 * S5_STATE
    u = jnp.concatenate(
        [jnp.concatenate([u_ref[b, i] for i in range(gc)], axis=1) for b in range(bsz)], axis=0)
    s_ref[...] = _dot(u, f_ref[0])

    row = lax.broadcasted_iota(jnp.int32, (8, lanes), 0)
    tab_a = tab_ref[0, 0:16, :]
    bc = lambda v: jnp.broadcast_to(v, (8, lanes))
    step_w = [[(bc(tab_a[8 * d + 2 * s:8 * d + 2 * s + 1]), bc(tab_a[8 * d + 2 * s + 1:8 * d + 2 * s + 2]))
               for s in range(4)] for d in range(2)]
    car = [(tab_ref[0, 16 + 16 * d:24 + 16 * d, :], tab_ref[0, 24 + 16 * d:32 + 16 * d, :])
           for d in range(2)]

    def shifted(v, sh, d):
        if d == 0:
            return jnp.where(row >= sh, pltpu.roll(v, sh, 0), 0.0)
        return jnp.where(row < 8 - sh, pltpu.roll(v, 8 - sh, 0), 0.0)

    def tile_scan(r0, hr, hi, d):
        c0 = 2 * lanes * d
        zr = s_ref[r0:r0 + 8, c0:c0 + lanes]
        zi = s_ref[r0:r0 + 8, c0 + lanes:c0 + 2 * lanes]
        for s, sh in enumerate((1, 2, 4)):
            wr, wi = step_w[d][s]
            sr, si = shifted(zr, sh, d), shifted(zi, sh, d)
            zr, zi = zr + wr * sr - wi * si, zi + wr * si + wi * sr
        pr, pi = car[d]
        xin_ref[r0:r0 + 8, c0:c0 + lanes] = pr * hr - pi * hi + shifted(zr, 1, d)
        xin_ref[r0:r0 + 8, c0 + lanes:c0 + 2 * lanes] = pr * hi + pi * hr + shifted(zi, 1, d)
        last = 0 if d else 7
        wr, wi = step_w[d][3]
        return (wr * hr - wi * hi + bc(zr[last:last + 1]), wr * hi + wi * hr + bc(zi[last:last + 1]))

    n_t = n_c // 8
    zero = jnp.zeros((8, lanes), F32)
    for b in range(bsz):
        fr, fi, br, bi = zero, zero, zero, zero
        for k in range(n_t):
            fr, fi = tile_scan(b * n_c + 8 * k, fr, fi, 0)
            br, bi = tile_scan(b * n_c + 8 * (n_t - 1 - k), br, bi, 1)

    y = _dot(u, m_ref[0]) + _dot(xin_ref[...].astype(BF16), e_ref[0])
    for b in range(bsz):
        for o in range(gc):
            y_ref[b, o] = y[b * n_c:(b + 1) * n_c, o * t:(o + 1) * t]


def _s5_scan(u_t, m, f, e, tab):
    bsz, width, seqlen = u_t.shape
    n_c = seqlen // S5_T
    gc = S5_GROUP_CH
    rows = bsz * n_c
    y = pl.pallas_call(
        _s5_kernel,
        grid=(S5_GROUPS,),
        in_specs=[
            pl.BlockSpec((bsz, gc, n_c, S5_T), lambda q: (0, q, 0, 0)),
            pl.BlockSpec((1, gc * S5_T, gc * S5_T), lambda q: (q, 0, 0)),
            pl.BlockSpec((1, gc * S5_T, 8 * S5_STATE), lambda q: (q, 0, 0)),
            pl.BlockSpec((1, 8 * S5_STATE, gc * S5_T), lambda q: (q, 0, 0)),
            pl.BlockSpec((1, 48, 2 * S5_STATE), lambda q: (q, 0, 0)),
        ],
        out_specs=pl.BlockSpec((bsz, gc, n_c, S5_T), lambda q: (0, q, 0, 0)),
        out_shape=jax.ShapeDtypeStruct((bsz, width, n_c, S5_T), F32),
        scratch_shapes=[pltpu.VMEM((rows, 8 * S5_STATE), F32), pltpu.VMEM((rows, 8 * S5_STATE), F32)],
        compiler_params=_params("parallel"),
    )(u_t.reshape(bsz, width, n_c, S5_T), m, f, e, tab)
    return y.reshape(bsz, width, seqlen)


def _gelu_tanh(v):
    return 0.5 * v * (1.0 + jnp.tanh(math.sqrt(2.0 / math.pi) * (v + 0.044715 * (v * v * v))))


def _outproj_kernel(yssd_ref, ys5_ref, u_ref, x_ref, g1_ref, sc2_ref, sh2_ref, d5_ref,
                    wglu_t_ref, bglu_ref, wout_ref, lng_ref, lnb_ref, x1_ref, h2_ref):
    y = ys5_ref[0] + u_ref[0].astype(F32) * d5_ref[...]
    y = _gelu_tanh(y)
    s5 = y * jax.nn.sigmoid(_dot(wglu_t_ref[...], y.astype(BF16)) + bglu_ref[...])
    mix = (_dot(yssd_ref[0], wout_ref[:SSD_WIDTH, :])
           + _dot_tn(s5.astype(BF16), wout_ref[SSD_WIDTH:, :]))
    x1 = _layer_norm(DEEPNORM_ALPHA * x_ref[0] + g1_ref[0] * mix, lng_ref[...], lnb_ref[...])
    x1_ref[0] = x1
    h2_ref[0] = (x1 * (1.0 + sc2_ref[0]) + sh2_ref[0]).astype(BF16)


def _outproj(yssd, ys5_t, u_t, x, g1, sc2, sh2, d5, wglu, bglu, wout, lng, lnb, tm):
    bsz, seqlen, _ = x.shape
    tok = lambda b, i: (b, i, 0)
    tok_t = lambda b, i: (b, 0, i)
    per_b = lambda b, i: (b, 0, 0)
    const = lambda b, i: (0, 0)
    row = lambda v: v.astype(F32).reshape(1, -1)
    col = lambda v: v.astype(F32).reshape(-1, 1)
    return pl.pallas_call(
        _outproj_kernel,
        grid=(bsz, seqlen // tm),
        in_specs=[
            pl.BlockSpec((1, tm, SSD_WIDTH), tok),
            pl.BlockSpec((1, S5_WIDTH, tm), tok_t),
            pl.BlockSpec((1, S5_WIDTH, tm), tok_t),
            pl.BlockSpec((1, tm, D_MODEL), tok),
            pl.BlockSpec((1, 1, D_MODEL), per_b),
            pl.BlockSpec((1, 1, D_MODEL), per_b),
            pl.BlockSpec((1, 1, D_MODEL), per_b),
            pl.BlockSpec((S5_WIDTH, 1), const),
            pl.BlockSpec((S5_WIDTH, S5_WIDTH), const),
            pl.BlockSpec((S5_WIDTH, 1), const),
            pl.BlockSpec((D_MODEL, D_MODEL), const),
            pl.BlockSpec((1, D_MODEL), const),
            pl.BlockSpec((1, D_MODEL), const),
        ],
        out_specs=[pl.BlockSpec((1, tm, D_MODEL), tok), pl.BlockSpec((1, tm, D_MODEL), tok)],
        out_shape=[jax.ShapeDtypeStruct((bsz, seqlen, D_MODEL), F32),
                   jax.ShapeDtypeStruct((bsz, seqlen, D_MODEL), BF16)],
        compiler_params=_params("parallel", "parallel"),
    )(yssd, ys5_t, u_t, x, g1, sc2, sh2, col(d5), wglu.T.astype(BF16), col(bglu), wout.astype(BF16),
      row(lng), row(lnb))


def _round_up(v, m):
    return (v + m - 1) // m * m


def _route_subtile(h2_s, wr_t, br, tri):
    s_len = h2_s.shape[0]
    logits = _dot_nt(wr_t, h2_s) + br
    mx = jnp.max(logits, axis=0, keepdims=True)
    ex = jnp.exp(logits - mx)
    probs = ex / jnp.sum(ex, axis=0, keepdims=True)
    row = lax.broadcasted_iota(jnp.int32, (N_EXPERTS, s_len), 0)
    n_grp = N_EXPERTS // EXPERTS_PER_GROUP
    gs = [jnp.max(probs[EXPERTS_PER_GROUP * g:EXPERTS_PER_GROUP * (g + 1)], axis=0, keepdims=True)
          for g in range(n_grp)]
    gmax = functools.reduce(jnp.maximum, gs)
    sel_grp = jnp.full((1, s_len), n_grp - 1, jnp.int32)
    for g in range(n_grp - 2, -1, -1):
        sel_grp = jnp.where(gs[g] == gmax, g, sel_grp)
    in_grp = (row // EXPERTS_PER_GROUP) == sel_grp
    masked = jnp.where(in_grp, probs, -1.0)
    m1 = jnp.max(masked, axis=0, keepdims=True)
    i1 = jnp.min(jnp.where(masked == m1, row, N_EXPERTS), axis=0, keepdims=True)
    is1 = row == i1
    masked2 = jnp.where(is1, -2.0, masked)
    m2 = jnp.max(masked2, axis=0, keepdims=True)
    i2 = jnp.min(jnp.where(masked2 == m2, row, N_EXPERTS), axis=0, keepdims=True)
    is2 = row == i2
    den = m1 + m2
    sel = jnp.where(is1, 1.0, jnp.where(is2, 1.0, 0.0))
    pos = _dot(sel.astype(BF16), tri)
    return is1, is2, m1 / den, m2 / den, sel, pos


def _copy_rows(src_ref, src0, dst_ref, dst0, n_chunks):
    def body(c, carry):
        so = pl.multiple_of(src0 + c * MOE_ALIGN, MOE_ALIGN)
        do = pl.multiple_of(dst0 + c * MOE_ALIGN, MOE_ALIGN)
        dst_ref[pl.ds(do, MOE_ALIGN), :] = src_ref[pl.ds(so, MOE_ALIGN), :]
        return carry
    lax.fori_loop(0, n_chunks, body, 0)


def _zero_rows(dst_ref, dst0, n_chunks):
    def body(c, carry):
        do = pl.multiple_of(dst0 + c * MOE_ALIGN, MOE_ALIGN)
        dst_ref[pl.ds(do, MOE_ALIGN), :] = jnp.zeros((MOE_ALIGN, dst_ref.shape[1]), dst_ref.dtype)
        return carry
    lax.fori_loop(0, n_chunks, body, 0)


def _moe_kernel(h2_ref, x1_ref, g2_ref, wr_t_ref, br_ref, tri_ref, wg_ref, wu_ref, wd_ref,
                lng_ref, lnb_ref, out_ref, xg_ref, cb_ref, route_ref, meta_ref):
    step = pl.program_id(2)
    tm = h2_ref.shape[1]
    s_len = x1_ref.shape[1]
    n_sub = tm // s_len
    cb_rows = cb_ref.shape[0]
    ne = N_EXPERTS
    seg_len, seg_cb, seg_xg, e_base, e_blk, sub_rows = 0, 64, 128, 192, 208, 224

    def onehot_rows(r0, n_rows, t1, t2, v1, v2):
        slot = (lax.broadcasted_iota(jnp.int32, (n_rows, s_len), 0) + r0).astype(F32)
        return jnp.where(slot == t1, v1, jnp.where(slot == t2, v2, 0.0))

    @pl.when(step == 0)
    def _dispatch():
        erow = lax.broadcasted_iota(jnp.int32, (ne, 1), 0)
        for s in range(n_sub):
            h2_s = h2_ref[0, s * s_len:(s + 1) * s_len, :]
            is1, is2, g1, g2, sel, pos = _route_subtile(h2_s, wr_t_ref[...], br_ref[...], tri_ref[...])
            base_vec = jnp.zeros((ne, 1), F32)
            run = jnp.int32(0)
            for e in range(ne):
                n16 = _round_up(jnp.sum(sel[e:e + 1, :]).astype(jnp.int32), MOE_ALIGN)
                meta_ref[seg_len + s * ne + e] = n16
                meta_ref[seg_cb + s * ne + e] = run
                base_vec = jnp.where(erow == e, run.astype(F32), base_vec)
                run = run + n16
            meta_ref[sub_rows + s] = run
            tgt = base_vec + pos
            route_ref[s, 0:1, :] = jnp.sum(jnp.where(is1, tgt, 0.0), axis=0, keepdims=True)
            route_ref[s, 1:2, :] = jnp.sum(jnp.where(is2, tgt, 0.0), axis=0, keepdims=True)
            route_ref[s, 2:3, :] = g1
            route_ref[s, 3:4, :] = g2
        run = jnp.int32(0)
        for e in range(ne):
            rows_e = jnp.int32(0)
            for s in range(n_sub):
                meta_ref[seg_xg + s * ne + e] = run + rows_e
                rows_e = rows_e + meta_ref[seg_len + s * ne + e]
            blocks = (rows_e + MOE_ROWS - 1) // MOE_ROWS
            meta_ref[e_base + e] = run
            meta_ref[e_blk + e] = blocks
            _zero_rows(xg_ref, run + rows_e, (blocks * MOE_ROWS - rows_e) // MOE_ALIGN)
            run = run + blocks * MOE_ROWS
        for s in range(n_sub):
            h2_s = h2_ref[0, s * s_len:(s + 1) * s_len, :]
            t1, t2 = route_ref[s, 0:1, :], route_ref[s, 1:2, :]
            for r0 in range(0, cb_rows, MOE_GATHER_ROWS):
                onehot = onehot_rows(r0, MOE_GATHER_ROWS, t1, t2, 1.0, 1.0).astype(BF16)
                cb_ref[r0:r0 + MOE_GATHER_ROWS, :] = _dot(onehot, h2_s).astype(BF16)
            for e in range(ne):
                _copy_rows(cb_ref, meta_ref[seg_cb + s * ne + e], xg_ref, meta_ref[seg_xg + s * ne + e],
                           meta_ref[seg_len + s * ne + e] // MOE_ALIGN)

    @pl.when(step < ne)
    def _expert():
        base = meta_ref[e_base + step]

        def block(j, carry):
            r0 = pl.multiple_of(base + j * MOE_ROWS, MOE_ROWS)
            xe = xg_ref[pl.ds(r0, MOE_ROWS), :]
            hid = (_silu(_dot(xe, wg_ref[0])) * _dot(xe, wu_ref[0])).astype(BF16)
            xg_ref[pl.ds(r0, MOE_ROWS), :] = _dot(hid, wd_ref[0]).astype(BF16)
            return carry

        lax.fori_loop(0, meta_ref[e_blk + step], block, 0)

    @pl.when(step >= ne)
    def _combine():
        s = step - ne
        for e in range(ne):
            _copy_rows(xg_ref, meta_ref[seg_xg + s * ne + e], cb_ref, meta_ref[seg_cb + s * ne + e],
                       meta_ref[seg_len + s * ne + e] // MOE_ALIGN)
        used = meta_ref[sub_rows + s]
        _zero_rows(cb_ref, used, (cb_rows - used) // MOE_ALIGN)
        t1, t2 = route_ref[s, 0:1, :], route_ref[s, 1:2, :]
        g1, g2 = route_ref[s, 2:3, :], route_ref[s, 3:4, :]
        acc = jnp.zeros((s_len, D_MODEL), F32)
        for r0 in range(0, cb_rows, MOE_GATHER_ROWS):
            weighted = onehot_rows(r0, MOE_GATHER_ROWS, t1, t2, g1, g2).astype(BF16)
            acc = acc + _dot_tn(weighted, cb_ref[r0:r0 + MOE_GATHER_ROWS, :])
        v = DEEPNORM_ALPHA * x1_ref[0] + g2_ref[0] * acc
        out_ref[0] = _layer_norm(v, lng_ref[...], lnb_ref[...])


def _moe(h2, x1, g2, wr_t, br, tri, wg, wu, wd, lng, lnb, tm, s_len):
    bsz, seqlen, _ = x1.shape
    n_sub = tm // s_len
    ne = N_EXPERTS
    cb_rows = _round_up(2 * s_len + ne * (MOE_ALIGN - 1), MOE_GATHER_ROWS)
    xg_rows = _round_up(2 * tm + n_sub * ne * (MOE_ALIGN - 1) + ne * (MOE_ROWS - 1), MOE_ROWS)
    tok = lambda b, i, st: (b, i, 0)
    sub = lambda b, i, st: (b, i * n_sub + jnp.clip(st - ne, 0, n_sub - 1), 0)
    per_b = lambda b, i, st: (b, 0, 0)
    const = lambda b, i, st: (0, 0)
    per_e = lambda b, i, st: (jnp.minimum(st, ne - 1), 0, 0)
    row = lambda v: v.astype(F32).reshape(1, -1)
    return pl.pallas_call(
        _moe_kernel,
        grid=(bsz, seqlen // tm, ne + n_sub),
        in_specs=[
            pl.BlockSpec((1, tm, D_MODEL), tok),
            pl.BlockSpec((1, s_len, D_MODEL), sub),
            pl.BlockSpec((1, 1, D_MODEL), per_b),
            pl.BlockSpec((ne, D_MODEL), const),
            pl.BlockSpec((ne, 1), const),
            pl.BlockSpec((s_len, s_len), const),
            pl.BlockSpec((1, D_MODEL, D_EXPERT), per_e),
            pl.BlockSpec((1, D_MODEL, D_EXPERT), per_e),
            pl.BlockSpec((1, D_EXPERT, D_MODEL), per_e),
            pl.BlockSpec((1, D_MODEL), const),
            pl.BlockSpec((1, D_MODEL), const),
        ],
        out_specs=pl.BlockSpec((1, s_len, D_MODEL), sub),
        out_shape=jax.ShapeDtypeStruct((bsz, seqlen, D_MODEL), F32),
        scratch_shapes=[pltpu.VMEM((xg_rows, D_MODEL), BF16), pltpu.VMEM((cb_rows, D_MODEL), BF16),
                        pltpu.VMEM((n_sub, 8, s_len), F32), pltpu.SMEM((256,), jnp.int32)],
        compiler_params=_params("parallel", "parallel", "arbitrary"),
    )(h2, x1, g2, wr_t, br, tri, wg, wu, wd, row(lng), row(lnb))


def _layer_weights(i, w):
    w_in = w['w_in'][i]
    n_dt = 2 * SSD_HEADS
    dt_lo = SSD_WIDTH + SSD_CONV_CH
    w_dt = w_in[:, dt_lo:dt_lo + n_dt]
    w_u = w_in[:, dt_lo + n_dt:]
    w_all = jnp.concatenate(
        [w_in[:, :dt_lo], w_dt, jnp.zeros((D_MODEL, DT_PAD - n_dt), w_in.dtype)], axis=1).astype(BF16)
    w_t = jnp.concatenate([w_u, w_dt], axis=1).T.astype(BF16)
    m, f, e, tab = _s5_tables(w['s5_lam_re'][i], w['s5_lam_im'][i], w['s5_log_step'][i],
                              w['s5_b_re'][i], w['s5_b_im'][i], w['s5_c_re'][i], w['s5_c_im'][i])
    return dict(
        w_all=w_all, w_t=w_t,
        s5=(m, f, e, tab),
        wg=w['w_gate'][i].astype(BF16), wu=w['w_up'][i].astype(BF16), wd=w['w_down'][i].astype(BF16),
    )


def _trunk(x, mods, w, layer_w, tiles):
    bsz, seqlen, _ = x.shape
    tm, tq, tmoe = tiles
    wr_t = w['w_router'].T.astype(BF16)
    br = w['b_router'].astype(F32).reshape(N_EXPERTS, 1)
    s_len = min(MOE_SUB, tmoe)
    ids = jnp.arange(s_len)
    tri = (ids[:, None] < ids[None, :]).astype(BF16)
    for i in range(DEPTH):
        lw = layer_w[i]
        sh1, sc1, g1, sh2, sc2, g2 = [v.reshape(bsz, 1, D_MODEL) for v in jnp.split(mods[i], 6, axis=-1)]
        z, xbc, dt, u_t, dt_t = _inproj(x, sc1, sh1, lw['w_all'], lw['w_t'], tm)
        xbc = _conv_silu(xbc, w['conv_w'][i], w['conv_b'][i], tm)
        y_f = _ssd_dir(xbc, dt, dt_t, w['ssd_a_log'][i], w['ssd_dt_bias'][i], False, tq)
        y_ssd = _ssd_dir(xbc, dt, dt_t, w['ssd_a_log'][i], w['ssd_dt_bias'][i], True, tq,
                         z=z, yf=y_f, d_skip=w['ssd_d'][i], norm_w=w['ssd_norm_w'][i])
        y_s5_t = _s5_scan(u_t, *lw['s5'])
        x1, h2 = _outproj(y_ssd, y_s5_t, u_t, x, g1, sc2, sh2, w['s5_d'][i], w['s5_w_glu'][i],
                          w['s5_b_glu'][i], w['w_out'][i], w['ln1_g'][i], w['ln1_b'][i], tm)
        x = _moe(h2, x1, g2, wr_t, br, tri, lw['wg'], lw['wu'], lw['wd'],
                 w['ln2_g'][i], w['ln2_b'][i], tmoe, s_len)
    return x


def _pick(n, pref):
    return pref if n % pref == 0 else n


def _run(x_groups, c_groups, w):
    rows = sum(c.shape[0] for c in c_groups)
    pad = (-rows) % 8
    c_all = jnp.concatenate(list(c_groups) + [jnp.zeros((pad, D_MODEL), F32)], axis=0).astype(F32)
    mods = _ada_mod(c_all, w['w_ada'].astype(F32), w['b_ada'].astype(F32))
    layer_w = [_layer_weights(i, w) for i in range(DEPTH)]
    outs = []
    r0 = 0
    for x, c in zip(x_groups, c_groups):
        bsz, seqlen, _ = x.shape
        assert seqlen % (8 * S5_T) == 0, "sequence length must be a multiple of 8 S5 chunks"
        tiles = (_pick(seqlen, 512), _pick(seqlen, 512), _pick(seqlen, 2048))
        outs.append(_trunk(x, mods[:, r0:r0 + bsz], w, layer_w, tiles))
        r0 += bsz
    return tuple(outs)


def kernel(x_prompt, x_sample, c_prompt, c_sample, w_ada, b_ada, w_in, conv_w, conv_b, ssd_a_log, ssd_dt_bias, ssd_d, ssd_norm_w, s5_lam_re, s5_lam_im, s5_log_step, s5_b_re, s5_b_im, s5_c_re, s5_c_im, s5_d, s5_w_glu, s5_b_glu, w_out, ln1_g, ln1_b, ln2_g, ln2_b, w_router, b_router, w_gate, w_up, w_down):
    w = dict(w_ada=w_ada, b_ada=b_ada, w_in=w_in, conv_w=conv_w, conv_b=conv_b, ssd_a_log=ssd_a_log,
             ssd_dt_bias=ssd_dt_bias, ssd_d=ssd_d, ssd_norm_w=ssd_norm_w, s5_lam_re=s5_lam_re,
             s5_lam_im=s5_lam_im, s5_log_step=s5_log_step, s5_b_re=s5_b_re, s5_b_im=s5_b_im,
             s5_c_re=s5_c_re, s5_c_im=s5_c_im, s5_d=s5_d, s5_w_glu=s5_w_glu, s5_b_glu=s5_b_glu,
             w_out=w_out, ln1_g=ln1_g, ln1_b=ln1_b, ln2_g=ln2_g, ln2_b=ln2_b, w_router=w_router,
             b_router=b_router, w_gate=w_gate, w_up=w_up, w_down=w_down)
    return _run((x_prompt, x_sample), (c_prompt, c_sample), w)
```

```python
import functools
import math

import jax
import jax.numpy as jnp
from jax import lax
from jax.experimental import pallas as pl
from jax.experimental.pallas import tpu as pltpu

F32 = jnp.float32
BF16 = jnp.bfloat16

D_MODEL = 1024
DEPTH = 4
SSD_WIDTH = 512
SSD_HEADDIM = 64
SSD_HEADS = 8
SSD_GROUPS = 2
SSD_STATE = 128
SSD_CONV = 5
SSD_CONV_CH = SSD_WIDTH + 2 * SSD_GROUPS * SSD_STATE
S5_WIDTH = 512
S5_GROUP_CH = 16
S5_GROUPS = 32
S5_STATE = 64
S5_MIN_DECAY = 1e-4
N_EXPERTS = 16
EXPERTS_PER_GROUP = 4
D_EXPERT = 512
DEEPNORM_ALPHA = (2 * DEPTH) ** 0.25
LN_EPS = 1e-5
RMS_EPS = 1e-5

SSD_Q = 128
S5_T = 128
PROJ_COLS = SSD_WIDTH + SSD_CONV_CH
DT_PAD = 128
CONV_HALO = 16
MOE_ROWS = 128
MOE_ALIGN = 16
MOE_GATHER_ROWS = 256
MOE_SUB = 512
VMEM_LIMIT = 56 * 1024 * 1024


def _dot(a, b):
    return jnp.dot(a, b, preferred_element_type=F32)


def _dot_nt(a, b):
    return lax.dot_general(a, b, (((1,), (1,)), ((), ())), preferred_element_type=F32)


def _dot_tn(a, b):
    return lax.dot_general(a, b, (((0,), (0,)), ((), ())), preferred_element_type=F32)


def _split_bf16(v):
    hi = v.astype(BF16)
    lo = (v - hi.astype(F32)).astype(BF16)
    return hi, lo


def _silu(v):
    return v * jax.nn.sigmoid(v)


def _layer_norm(v, g, b):
    mu = jnp.mean(v, axis=-1, keepdims=True)
    vc = v - mu
    var = jnp.mean(vc * vc, axis=-1, keepdims=True)
    return vc * lax.rsqrt(var + LN_EPS) * g + b


def _params(*sem):
    return pltpu.CompilerParams(dimension_semantics=sem, vmem_limit_bytes=VMEM_LIMIT)


def _ada_kernel(c_ref, w_ref, b_ref, o_ref):
    ca = _silu(c_ref[...])
    c_hi, c_lo = _split_bf16(ca)
    w_hi, w_lo = _split_bf16(w_ref[0])
    acc = _dot(c_hi, w_hi) + _dot(c_lo, w_hi) + _dot(c_hi, w_lo)
    o_ref[0] = acc + b_ref[0]


def _ada_mod(c_all, w_ada, b_ada):
    rows = c_all.shape[0]
    n_out = w_ada.shape[-1]
    bn = 1536
    return pl.pallas_call(
        _ada_kernel,
        grid=(DEPTH, n_out // bn),
        in_specs=[
            pl.BlockSpec((rows, D_MODEL), lambda l, j: (0, 0)),
            pl.BlockSpec((1, D_MODEL, bn), lambda l, j: (l, 0, j)),
            pl.BlockSpec((1, 1, bn), lambda l, j: (l, 0, j)),
        ],
        out_specs=pl.BlockSpec((1, rows, bn), lambda l, j: (l, 0, j)),
        out_shape=jax.ShapeDtypeStruct((DEPTH, rows, n_out), F32),
        compiler_params=_params("arbitrary", "arbitrary"),
    )(c_all, w_ada, b_ada.reshape(DEPTH, 1, n_out))


def _inproj_kernel(x_ref, sc_ref, sh_ref, w_ref, w_t_ref,
                   z_ref, xbc_ref, dt_ref, u_t_ref, dt_t_ref):
    h = (x_ref[0] * (1.0 + sc_ref[0]) + sh_ref[0]).astype(BF16)
    p = _dot(h, w_ref[...])
    z_ref[0] = p[:, :SSD_WIDTH].astype(BF16)
    xbc_ref[0] = p[:, SSD_WIDTH:PROJ_COLS].astype(BF16)
    dt_ref[0] = p[:, PROJ_COLS:PROJ_COLS + 2 * SSD_HEADS]
    p_t = _dot_nt(w_t_ref[...], h)
    u_t_ref[0] = p_t[:S5_WIDTH].astype(BF16)
    dt_t_ref[0] = p_t[S5_WIDTH:]


def _inproj(x, sc, sh, w_all, w_t, tm):
    bsz, seqlen, _ = x.shape
    tok = lambda b, i: (b, i, 0)
    tok_t = lambda b, i: (b, 0, i)
    per_b = lambda b, i: (b, 0, 0)
    const = lambda b, i: (0, 0)
    return pl.pallas_call(
        _inproj_kernel,
        grid=(bsz, seqlen // tm),
        in_specs=[
            pl.BlockSpec((1, tm, D_MODEL), tok),
            pl.BlockSpec((1, 1, D_MODEL), per_b),
            pl.BlockSpec((1, 1, D_MODEL), per_b),
            pl.BlockSpec(w_all.shape, const),
            pl.BlockSpec(w_t.shape, const),
        ],
        out_specs=[
            pl.BlockSpec((1, tm, SSD_WIDTH), tok),
            pl.BlockSpec((1, tm, SSD_CONV_CH), tok),
            pl.BlockSpec((1, tm, 2 * SSD_HEADS), tok),
            pl.BlockSpec((1, S5_WIDTH, tm), tok_t),
            pl.BlockSpec((1, 2 * SSD_HEADS, tm), tok_t),
        ],
        out_shape=[
            jax.ShapeDtypeStruct((bsz, seqlen, SSD_WIDTH), BF16),
            jax.ShapeDtypeStruct((bsz, seqlen, SSD_CONV_CH), BF16),
            jax.ShapeDtypeStruct((bsz, seqlen, 2 * SSD_HEADS), F32),
            jax.ShapeDtypeStruct((bsz, S5_WIDTH, seqlen), BF16),
            jax.ShapeDtypeStruct((bsz, 2 * SSD_HEADS, seqlen), F32),
        ],
        compiler_params=_params("parallel", "parallel"),
    )(x, sc, sh, w_all, w_t)


def _conv_kernel(xm_ref, xl_ref, xr_ref, w_ref, b_ref, o_ref, ext_ref):
    i = pl.program_id(1)
    n = pl.num_programs(1)
    tc = xm_ref.shape[1]
    has_left = (i > 0).astype(F32)
    has_right = (i < n - 1).astype(F32)
    ext_ref[0:CONV_HALO, :] = xl_ref[0].astype(F32) * has_left
    ext_ref[CONV_HALO:CONV_HALO + tc, :] = xm_ref[0].astype(F32)
    ext_ref[CONV_HALO + tc:2 * CONV_HALO + tc, :] = xr_ref[0].astype(F32) * has_right
    acc = jnp.zeros((tc, SSD_CONV_CH), F32) + b_ref[...]
    for k in range(SSD_CONV):
        start = CONV_HALO - SSD_CONV // 2 + k
        acc = acc + ext_ref[start:start + tc, :] * w_ref[k:k + 1, :]
    o_ref[0] = _silu(acc).astype(BF16)


def _conv_silu(xbc, conv_w, conv_b, tc):
    bsz, seqlen, ch = xbc.shape
    per_blk = tc // CONV_HALO
    n_halo = seqlen // CONV_HALO
    w_pad = jnp.zeros((8, ch), F32).at[:SSD_CONV].set(conv_w.astype(F32))
    return pl.pallas_call(
        _conv_kernel,
        grid=(bsz, seqlen // tc),
        in_specs=[
            pl.BlockSpec((1, tc, ch), lambda b, i: (b, i, 0)),
            pl.BlockSpec((1, CONV_HALO, ch), lambda b, i: (b, jnp.maximum(i * per_blk - 1, 0), 0)),
            pl.BlockSpec((1, CONV_HALO, ch),
                         lambda b, i: (b, jnp.minimum((i + 1) * per_blk, n_halo - 1), 0)),
            pl.BlockSpec((8, ch), lambda b, i: (0, 0)),
            pl.BlockSpec((1, ch), lambda b, i: (0, 0)),
        ],
        out_specs=pl.BlockSpec((1, tc, ch), lambda b, i: (b, i, 0)),
        out_shape=jax.ShapeDtypeStruct((bsz, seqlen, ch), BF16),
        scratch_shapes=[pltpu.VMEM((tc + 2 * CONV_HALO, ch), F32)],
        compiler_params=_params("parallel", "parallel"),
    )(xbc, xbc, xbc, w_pad, conv_b.astype(F32).reshape(1, ch))


def _softplus(v):
    return jnp.maximum(v, 0.0) + jnp.log1p(jnp.exp(-jnp.abs(v)))


def _head_indicator(width):
    head = lax.broadcasted_iota(jnp.int32, (SSD_HEADS, SSD_HEADS * width), 0)
    lane = lax.broadcasted_iota(jnp.int32, (SSD_HEADS, SSD_HEADS * width), 1)
    return jnp.where(lane // width == head, 1.0, 0.0).astype(BF16)


def _ssd_chunk(xbc, dt_raw, dt_raw_t, a_row, a_col, bias_row, bias_col, state_ref, rev):
    q = xbc.shape[0]
    x = xbc[:, :SSD_WIDTH]
    b_mat = xbc[:, SSD_WIDTH:SSD_WIDTH + SSD_GROUPS * SSD_STATE]
    c_mat = xbc[:, SSD_WIDTH + SSD_GROUPS * SSD_STATE:]
    dts = _softplus(dt_raw + bias_row)
    dts_t = _softplus(dt_raw_t + bias_col)
    dta = dts * a_row
    dta_t = dts_t * a_col

    row = lax.broadcasted_iota(jnp.int32, (q, q), 0)
    col = lax.broadcasted_iota(jnp.int32, (q, q), 1)
    lower = col <= row
    upper = col >= row
    tri_lower = jnp.where(lower, 1.0, 0.0).astype(BF16)
    tri_upper = jnp.where(upper, 1.0, 0.0).astype(BF16)
    tri_c, tri_r, mask = (tri_upper, tri_lower, upper) if rev else (tri_lower, tri_upper, lower)

    d_hi, d_lo = _split_bf16(dta)
    acc = _dot(tri_c, d_hi) + _dot(tri_c, d_lo)
    t_hi, t_lo = _split_bf16(dta_t)
    acc_t = _dot(t_hi, tri_r) + _dot(t_lo, tri_r)

    lane128 = lax.broadcasted_iota(jnp.int32, (q, 128), 1)
    x32 = x.astype(F32)
    cb = [_dot_nt(c_mat[:, g * SSD_STATE:(g + 1) * SSD_STATE],
                  b_mat[:, g * SSD_STATE:(g + 1) * SSD_STATE]) for g in range(SSD_GROUPS)]
    a_hi, a_lo = _split_bf16(acc)
    ind_q = _head_indicator(q)
    t_row = jnp.concatenate([acc_t[h:h + 1, :] for h in range(SSD_HEADS)], axis=1)
    seg_all = _dot(a_hi, ind_q) + _dot(a_lo, ind_q) - t_row
    ys = []
    for k in range(SSD_HEADS // 2):
        m_pair = []
        for hh in range(2):
            h = 2 * k + hh
            g = h // (SSD_HEADS // SSD_GROUPS)
            seg = seg_all[:, h * q:(h + 1) * q]
            dec = jnp.where(mask, jnp.exp(jnp.minimum(seg, 0.0)), 0.0)
            m_pair.append((cb[g] * dec * dts_t[h:h + 1, :]).astype(BF16))
        xp = x32[:, 128 * k:128 * (k + 1)]
        x_blk = jnp.concatenate([jnp.where(lane128 < SSD_HEADDIM, xp, 0.0).astype(BF16),
                                 jnp.where(lane128 < SSD_HEADDIM, 0.0, xp).astype(BF16)], axis=0)
        ys.append(_dot(jnp.concatenate(m_pair, axis=1), x_blk))
    y = jnp.concatenate(ys, axis=1)

    end = acc[0:1, :] if rev else acc[q - 1:q, :]
    ind_p = _head_indicator(SSD_HEADDIM)
    y_scale = _dot(jnp.exp(acc).astype(BF16), ind_p)
    x_w = _dot((dts * jnp.exp(end - acc)).astype(BF16), ind_p)
    c_hi, c_lo = _split_bf16(jnp.broadcast_to(jnp.exp(end), (8, SSD_HEADS)))
    carry = (_dot(c_hi, ind_p) + _dot(c_lo, ind_p))[0:1, :]
    xw = (x32 * x_w).astype(BF16)
    half = SSD_WIDTH // SSD_GROUPS
    y_off = []
    for g in range(SSD_GROUPS):
        st = state_ref[g]
        y_off.append(_dot(c_mat[:, g * SSD_STATE:(g + 1) * SSD_STATE], st.astype(BF16)))
        state_ref[g] = st * carry[:, g * half:(g + 1) * half] + _dot_tn(
            b_mat[:, g * SSD_STATE:(g + 1) * SSD_STATE], xw[:, g * half:(g + 1) * half])
    return y + jnp.concatenate(y_off, axis=1) * y_scale


def _ssd_dir_kernel(*refs, rev):
    if rev:
        (xbc_ref, dt_ref, dt_t_ref, alog_ref, alog_t_ref, bias_ref, bias_t_ref,
         z_ref, yf_ref, dskip_ref, nw_ref, o_ref, state_ref) = refs
    else:
        (xbc_ref, dt_ref, dt_t_ref, alog_ref, alog_t_ref, bias_ref, bias_t_ref,
         o_ref, state_ref) = refs

    @pl.when(pl.program_id(1) == 0)
    def _():
        state_ref[...] = jnp.zeros(state_ref.shape, F32)

    d = 1 if rev else 0
    hs = slice(d * SSD_HEADS, (d + 1) * SSD_HEADS)
    a_row = -jnp.exp(alog_ref[:, hs])
    a_col = -jnp.exp(alog_t_ref[hs, :])
    bias_row = bias_ref[:, hs]
    bias_col = bias_t_ref[hs, :]
    tq = xbc_ref.shape[1]
    n_chunks = tq // SSD_Q
    order = range(n_chunks - 1, -1, -1) if rev else range(n_chunks)
    for ci in order:
        rs = slice(ci * SSD_Q, (ci + 1) * SSD_Q)
        xbc = xbc_ref[0, rs, :]
        y = _ssd_chunk(xbc, dt_ref[0, rs, hs], dt_t_ref[0, hs, rs], a_row, a_col,
                       bias_row, bias_col, state_ref, rev)
        if rev:
            y = y + yf_ref[0, rs, :] + xbc[:, :SSD_WIDTH].astype(F32) * dskip_ref[...]
            y = y * _silu(z_ref[0, rs, :].astype(F32))
            y = y * lax.rsqrt(jnp.mean(y * y, axis=-1, keepdims=True) + RMS_EPS) * nw_ref[...]
            o_ref[0, rs, :] = y.astype(BF16)
        else:
            o_ref[0, rs, :] = y


def _ssd_dir(xbc, dt, dt_t, a_log, dt_bias, rev, tq, z=None, yf=None, d_skip=None, norm_w=None):
    bsz, seqlen, _ = xbc.shape
    nblk = seqlen // tq
    if rev:
        tok = lambda b, i: (b, nblk - 1 - i, 0)
        tok_t = lambda b, i: (b, 0, nblk - 1 - i)
    else:
        tok = lambda b, i: (b, i, 0)
        tok_t = lambda b, i: (b, 0, i)
    const = lambda b, i: (0, 0)
    nh2 = 2 * SSD_HEADS
    a_flat = a_log.astype(F32).reshape(1, nh2)
    bias_flat = dt_bias.astype(F32).reshape(1, nh2)
    args = [xbc, dt, dt_t, a_flat, a_flat.reshape(nh2, 1), bias_flat, bias_flat.reshape(nh2, 1)]
    in_specs = [
        pl.BlockSpec((1, tq, SSD_CONV_CH), tok),
        pl.BlockSpec((1, tq, nh2), tok),
        pl.BlockSpec((1, nh2, tq), tok_t),
        pl.BlockSpec((1, nh2), const),
        pl.BlockSpec((nh2, 1), const),
        pl.BlockSpec((1, nh2), const),
        pl.BlockSpec((nh2, 1), const),
    ]
    if rev:
        args += [z, yf, jnp.repeat(d_skip.astype(F32), SSD_HEADDIM).reshape(1, SSD_WIDTH),
                 norm_w.astype(F32).reshape(1, SSD_WIDTH)]
        in_specs += [
            pl.BlockSpec((1, tq, SSD_WIDTH), tok),
            pl.BlockSpec((1, tq, SSD_WIDTH), tok),
            pl.BlockSpec((1, SSD_WIDTH), const),
            pl.BlockSpec((1, SSD_WIDTH), const),
        ]
    return pl.pallas_call(
        functools.partial(_ssd_dir_kernel, rev=rev),
        grid=(bsz, nblk),
        in_specs=in_specs,
        out_specs=pl.BlockSpec((1, tq, SSD_WIDTH), tok),
        out_shape=jax.ShapeDtypeStruct((bsz, seqlen, SSD_WIDTH), BF16 if rev else F32),
        scratch_shapes=[pltpu.VMEM((SSD_GROUPS, SSD_STATE, SSD_WIDTH // SSD_GROUPS), F32)],
        compiler_params=_params("parallel", "arbitrary"),
    )(*args)


def _s5_toeplitz_kernel(k_ref, m_ref):
    t = S5_T
    gc = S5_GROUP_CH
    def rows_of_channel(i, carry):
        r0 = pl.multiple_of(i * t, t)
        for o in range(gc):
            taps = jnp.broadcast_to(k_ref[0, i, o:o + 1, :], (t, 2 * t))
            skew = pltpu.roll(taps, t + 1, 1, stride=1, stride_axis=0)
            m_ref[0, pl.ds(r0, t), o * t:(o + 1) * t] = skew[:, :t].astype(BF16)
        return carry

    lax.fori_loop(0, gc, rows_of_channel, 0)


def _s5_toeplitz(kall):
    g, gc, _, taps = kall.shape
    n = gc * S5_T
    return pl.pallas_call(
        _s5_toeplitz_kernel,
        grid=(g,),
        in_specs=[pl.BlockSpec((1, gc, gc, taps), lambda q: (q, 0, 0, 0))],
        out_specs=pl.BlockSpec((1, n, n), lambda q: (q, 0, 0)),
        out_shape=jax.ShapeDtypeStruct((g, n, n), BF16),
        compiler_params=_params("parallel"),
    )(kall)


def _s5_tables(lam_re, lam_im, log_step, b_re, b_im, c_re, c_im):
    hp = lax.Precision.HIGHEST
    t = S5_T
    g, p, gc = S5_GROUPS, S5_STATE, S5_GROUP_CH
    lr = jnp.minimum(lam_re.astype(F32), -S5_MIN_DECAY)
    li = lam_im.astype(F32)
    step = jnp.exp(log_step.astype(F32))[..., None]
    ar, ai = lr * step, li * step

    def power(n):
        mag = jnp.exp(n * ar)
        return mag * jnp.cos(n * ai), mag * jnp.sin(n * ai)

    lbr, lbi = power(1.0)
    den = lr * lr + li * li
    qr = ((lbr - 1.0) * lr + lbi * li) / den
    qi = (lbi * lr - (lbr - 1.0) * li) / den
    bre, bim = b_re.astype(F32)[None], b_im.astype(F32)[None]
    bbr = qr[..., None] * bre - qi[..., None] * bim
    bbi = qr[..., None] * bim + qi[..., None] * bre
    cr, ci = c_re.astype(F32), c_im.astype(F32)

    ks = jnp.arange(t + 1, dtype=F32)[:, None, None, None]
    pwr, pwi = power(ks)
    cpr = cr[None] * pwr[:, :, :, None, :] - ci[None] * pwi[:, :, :, None, :]
    cpi = cr[None] * pwi[:, :, :, None, :] + ci[None] * pwr[:, :, :, None, :]
    kern = (jnp.einsum('kdgop,dgpi->kdgoi', cpr, bbr, precision=hp)
            - jnp.einsum('kdgop,dgpi->kdgoi', cpi, bbi, precision=hp))
    kf, kb = kern[:t, 0], kern[:t, 1]
    kall = jnp.concatenate([kb[:0:-1], (kf[0] + kb[0])[None], kf[1:], jnp.zeros_like(kf[:1])], axis=0)
    m = _s5_toeplitz(jnp.transpose(kall, (1, 3, 2, 0)))

    pad = lambda v, axis: jnp.concatenate([v, jnp.zeros_like(v)], axis=axis)

    def state_map(pr, pi, d):
        fr = pr[..., None] * bbr[d][None] - pi[..., None] * bbi[d][None]
        fi = pr[..., None] * bbi[d][None] + pi[..., None] * bbr[d][None]
        to_rows = lambda v: pad(jnp.transpose(v, (1, 3, 0, 2)).reshape(g, gc * t, p), 2)
        return to_rows(fr), to_rows(fi)

    f_fr, f_fi = state_map(pwr[:t, 0][::-1], pwi[:t, 0][::-1], 0)
    f_br, f_bi = state_map(pwr[:t, 1], pwi[:t, 1], 1)
    f = jnp.concatenate([f_fr, f_fi, f_br, f_bi], axis=2)

    def out_map(cpr_d, cpi_d):
        to_cols = lambda v: pad(jnp.transpose(v, (1, 3, 2, 0)).reshape(g, p, gc * t), 1)
        return to_cols(cpr_d), to_cols(-cpi_d)

    e_fr, e_fi = out_map(cpr[1:t + 1, 0], cpi[1:t + 1, 0])
    e_br, e_bi = out_map(cpr[1:t + 1, 1][::-1], cpi[1:t + 1, 1][::-1])
    e = jnp.concatenate([e_fr, e_fi, e_br, e_bi], axis=1)

    step_r, step_i = power(jnp.array([1.0, 2.0, 4.0, 8.0], F32)[:, None, None, None] * t)
    rows_a = []
    for d in range(2):
        for s in range(4):
            rows_a += [step_r[s, d], step_i[s, d]]
    r8 = jnp.arange(8, dtype=F32)[:, None, None, None] * t
    car_r, car_i = power(r8)
    rows = jnp.concatenate([jnp.stack(rows_a, axis=0), car_r[:, 0], car_i[:, 0],
                            car_r[::-1, 1], car_i[::-1, 1]], axis=0)
    tab = pad(jnp.transpose(rows, (1, 0, 2)), 2)
    return m, f.astype(BF16), e.astype(BF16), tab


def _s5_kernel(u_ref, m_ref, f_ref, e_ref, tab_ref, y_ref, s_ref, xin_ref):
    bsz, gc, n_c, t = u_ref.shape
    lanes = 2 * S5_STATE
    u = jnp.concatenate(
        [jnp.concatenate([u_ref[b, i] for i in range(gc)], axis=1) for b in range(bsz)], axis=0)
    s_ref[...] = _dot(u, f_ref[0])

    row = lax.broadcasted_iota(jnp.int32, (8, lanes), 0)
    tab_a = tab_ref[0, 0:16, :]
    bc = lambda v: jnp.broadcast_to(v, (8, lanes))
    step_w = [[(bc(tab_a[8 * d + 2 * s:8 * d + 2 * s + 1]), bc(tab_a[8 * d + 2 * s + 1:8 * d + 2 * s + 2]))
               for s in range(4)] for d in range(2)]
    car = [(tab_ref[0, 16 + 16 * d:24 + 16 * d, :], tab_ref[0, 24 + 16 * d:32 + 16 * d, :])
           for d in range(2)]

    def shifted(v, sh, d):
        if d == 0:
            return jnp.where(row >= sh, pltpu.roll(v, sh, 0), 0.0)
        return jnp.where(row < 8 - sh, pltpu.roll(v, 8 - sh, 0), 0.0)

    def tile_scan(r0, hr, hi, d):
        c0 = 2 * lanes * d
        zr = s_ref[r0:r0 + 8, c0:c0 + lanes]
        zi = s_ref[r0:r0 + 8, c0 + lanes:c0 + 2 * lanes]
        for s, sh in enumerate((1, 2, 4)):
            wr, wi = step_w[d][s]
            sr, si = shifted(zr, sh, d), shifted(zi, sh, d)
            zr, zi = zr + wr * sr - wi * si, zi + wr * si + wi * sr
        pr, pi = car[d]
        xin_ref[r0:r0 + 8, c0:c0 + lanes] = pr * hr - pi * hi + shifted(zr, 1, d)
        xin_ref[r0:r0 + 8, c0 + lanes:c0 + 2 * lanes] = pr * hi + pi * hr + shifted(zi, 1, d)
        last = 0 if d else 7
        wr, wi = step_w[d][3]
        return (wr * hr - wi * hi + bc(zr[last:last + 1]), wr * hi + wi * hr + bc(zi[last:last + 1]))

    n_t = n_c // 8
    zero = jnp.zeros((8, lanes), F32)
    for b in range(bsz):
        fr, fi, br, bi = zero, zero, zero, zero
        for k in range(n_t):
            fr, fi = tile_scan(b * n_c + 8 * k, fr, fi, 0)
            br, bi = tile_scan(b * n_c + 8 * (n_t - 1 - k), br, bi, 1)

    y = _dot(u, m_ref[0]) + _dot(xin_ref[...].astype(BF16), e_ref[0])
    for b in range(bsz):
        for o in range(gc):
            y_ref[b, o] = y[b * n_c:(b + 1) * n_c, o * t:(o + 1) * t]


def _s5_scan(u_t, m, f, e, tab):
    bsz, width, seqlen = u_t.shape
    n_c = seqlen // S5_T
    gc = S5_GROUP_CH
    rows = bsz * n_c
    y = pl.pallas_call(
        _s5_kernel,
        grid=(S5_GROUPS,),
        in_specs=[
            pl.BlockSpec((bsz, gc, n_c, S5_T), lambda q: (0, q, 0, 0)),
            pl.BlockSpec((1, gc * S5_T, gc * S5_T), lambda q: (q, 0, 0)),
            pl.BlockSpec((1, gc * S5_T, 8 * S5_STATE), lambda q: (q, 0, 0)),
            pl.BlockSpec((1, 8 * S5_STATE, gc * S5_T), lambda q: (q, 0, 0)),
            pl.BlockSpec((1, 48, 2 * S5_STATE), lambda q: (q, 0, 0)),
        ],
        out_specs=pl.BlockSpec((bsz, gc, n_c, S5_T), lambda q: (0, q, 0, 0)),
        out_shape=jax.ShapeDtypeStruct((bsz, width, n_c, S5_T), F32),
        scratch_shapes=[pltpu.VMEM((rows, 8 * S5_STATE), F32), pltpu.VMEM((rows, 8 * S5_STATE), F32)],
        compiler_params=_params("parallel"),
    )(u_t.reshape(bsz, width, n_c, S5_T), m, f, e, tab)
    return y.reshape(bsz, width, seqlen)


def _gelu_tanh(v):
    return 0.5 * v * (1.0 + jnp.tanh(math.sqrt(2.0 / math.pi) * (v + 0.044715 * (v * v * v))))


def _outproj_kernel(yssd_ref, ys5_ref, u_ref, x_ref, g1_ref, sc2_ref, sh2_ref, d5_ref,
                    wglu_t_ref, bglu_ref, wout_ref, lng_ref, lnb_ref, x1_ref, h2_ref):
    y = ys5_ref[0] + u_ref[0].astype(F32) * d5_ref[...]
    y = _gelu_tanh(y)
    s5 = y * jax.nn.sigmoid(_dot(wglu_t_ref[...], y.astype(BF16)) + bglu_ref[...])
    mix = (_dot(yssd_ref[0], wout_ref[:SSD_WIDTH, :])
           + _dot_tn(s5.astype(BF16), wout_ref[SSD_WIDTH:, :]))
    x1 = _layer_norm(DEEPNORM_ALPHA * x_ref[0] + g1_ref[0] * mix, lng_ref[...], lnb_ref[...])
    x1_ref[0] = x1
    h2_ref[0] = (x1 * (1.0 + sc2_ref[0]) + sh2_ref[0]).astype(BF16)


def _outproj(yssd, ys5_t, u_t, x, g1, sc2, sh2, d5, wglu, bglu, wout, lng, lnb, tm):
    bsz, seqlen, _ = x.shape
    tok = lambda b, i: (b, i, 0)
    tok_t = lambda b, i: (b, 0, i)
    per_b = lambda b, i: (b, 0, 0)
    const = lambda b, i: (0, 0)
    row = lambda v: v.astype(F32).reshape(1, -1)
    col = lambda v: v.astype(F32).reshape(-1, 1)
    return pl.pallas_call(
        _outproj_kernel,
        grid=(bsz, seqlen // tm),
        in_specs=[
            pl.BlockSpec((1, tm, SSD_WIDTH), tok),
            pl.BlockSpec((1, S5_WIDTH, tm), tok_t),
            pl.BlockSpec((1, S5_WIDTH, tm), tok_t),
            pl.BlockSpec((1, tm, D_MODEL), tok),
            pl.BlockSpec((1, 1, D_MODEL), per_b),
            pl.BlockSpec((1, 1, D_MODEL), per_b),
            pl.BlockSpec((1, 1, D_MODEL), per_b),
            pl.BlockSpec((S5_WIDTH, 1), const),
            pl.BlockSpec((S5_WIDTH, S5_WIDTH), const),
            pl.BlockSpec((S5_WIDTH, 1), const),
            pl.BlockSpec((D_MODEL, D_MODEL), const),
            pl.BlockSpec((1, D_MODEL), const),
            pl.BlockSpec((1, D_MODEL), const),
        ],
        out_specs=[pl.BlockSpec((1, tm, D_MODEL), tok), pl.BlockSpec((1, tm, D_MODEL), tok)],
        out_shape=[jax.ShapeDtypeStruct((bsz, seqlen, D_MODEL), F32),
                   jax.ShapeDtypeStruct((bsz, seqlen, D_MODEL), BF16)],
        compiler_params=_params("parallel", "parallel"),
    )(yssd, ys5_t, u_t, x, g1, sc2, sh2, col(d5), wglu.T.astype(BF16), col(bglu), wout.astype(BF16),
      row(lng), row(lnb))


def _round_up(v, m):
    return (v + m - 1) // m * m


def _route_subtile(h2_s, wr_t, br, tri):
    s_len = h2_s.shape[0]
    logits = _dot_nt(wr_t, h2_s) + br
    mx = jnp.max(logits, axis=0, keepdims=True)
    ex = jnp.exp(logits - mx)
    probs = ex / jnp.sum(ex, axis=0, keepdims=True)
    row = lax.broadcasted_iota(jnp.int32, (N_EXPERTS, s_len), 0)
    n_grp = N_EXPERTS // EXPERTS_PER_GROUP
    gs = [jnp.max(probs[EXPERTS_PER_GROUP * g:EXPERTS_PER_GROUP * (g + 1)], axis=0, keepdims=True)
          for g in range(n_grp)]
    gmax = functools.reduce(jnp.maximum, gs)
    sel_grp = jnp.full((1, s_len), n_grp - 1, jnp.int32)
    for g in range(n_grp - 2, -1, -1):
        sel_grp = jnp.where(gs[g] == gmax, g, sel_grp)
    in_grp = (row // EXPERTS_PER_GROUP) == sel_grp
    masked = jnp.where(in_grp, probs, -1.0)
    m1 = jnp.max(masked, axis=0, keepdims=True)
    i1 = jnp.min(jnp.where(masked == m1, row, N_EXPERTS), axis=0, keepdims=True)
    is1 = row == i1
    masked2 = jnp.where(is1, -2.0, masked)
    m2 = jnp.max(masked2, axis=0, keepdims=True)
    i2 = jnp.min(jnp.where(masked2 == m2, row, N_EXPERTS), axis=0, keepdims=True)
    is2 = row == i2
    den = m1 + m2
    sel = jnp.where(is1, 1.0, jnp.where(is2, 1.0, 0.0))
    pos = _dot(sel.astype(BF16), tri)
    return is1, is2, m1 / den, m2 / den, sel, pos


def _copy_rows(src_ref, src0, dst_ref, dst0, n_chunks):
    def body(c, carry):
        so = pl.multiple_of(src0 + c * MOE_ALIGN, MOE_ALIGN)
        do = pl.multiple_of(dst0 + c * MOE_ALIGN, MOE_ALIGN)
        dst_ref[pl.ds(do, MOE_ALIGN), :] = src_ref[pl.ds(so, MOE_ALIGN), :]
        return carry
    lax.fori_loop(0, n_chunks, body, 0)


def _zero_rows(dst_ref, dst0, n_chunks):
    def body(c, carry):
        do = pl.multiple_of(dst0 + c * MOE_ALIGN, MOE_ALIGN)
        dst_ref[pl.ds(do, MOE_ALIGN), :] = jnp.zeros((MOE_ALIGN, dst_ref.shape[1]), dst_ref.dtype)
        return carry
    lax.fori_loop(0, n_chunks, body, 0)


def _moe_kernel(h2_ref, x1_ref, g2_ref, wr_t_ref, br_ref, tri_ref, wg_ref, wu_ref, wd_ref,
                lng_ref, lnb_ref, out_ref, xg_ref, cb_ref, route_ref, meta_ref):
    step = pl.program_id(2)
    tm = h2_ref.shape[1]
    s_len = x1_ref.shape[1]
    n_sub = tm // s_len
    cb_rows = cb_ref.shape[0]
    ne = N_EXPERTS
    seg_len, seg_cb, seg_xg, e_base, e_blk, sub_rows = 0, 64, 128, 192, 208, 224

    def onehot_rows(r0, n_rows, t1, t2, v1, v2):
        slot = (lax.broadcasted_iota(jnp.int32, (n_rows, s_len), 0) + r0).astype(F32)
        return jnp.where(slot == t1, v1, jnp.where(slot == t2, v2, 0.0))

    @pl.when(step == 0)
    def _dispatch():
        erow = lax.broadcasted_iota(jnp.int32, (ne, 1), 0)
        for s in range(n_sub):
            h2_s = h2_ref[0, s * s_len:(s + 1) * s_len, :]
            is1, is2, g1, g2, sel, pos = _route_subtile(h2_s, wr_t_ref[...], br_ref[...], tri_ref[...])
            base_vec = jnp.zeros((ne, 1), F32)
            run = jnp.int32(0)
            for e in range(ne):
                n16 = _round_up(jnp.sum(sel[e:e + 1, :]).astype(jnp.int32), MOE_ALIGN)
                meta_ref[seg_len + s * ne + e] = n16
                meta_ref[seg_cb + s * ne + e] = run
                base_vec = jnp.where(erow == e, run.astype(F32), base_vec)
                run = run + n16
            meta_ref[sub_rows + s] = run
            tgt = base_vec + pos
            route_ref[s, 0:1, :] = jnp.sum(jnp.where(is1, tgt, 0.0), axis=0, keepdims=True)
            route_ref[s, 1:2, :] = jnp.sum(jnp.where(is2, tgt, 0.0), axis=0, keepdims=True)
            route_ref[s, 2:3, :] = g1
            route_ref[s, 3:4, :] = g2
        run = jnp.int32(0)
        for e in range(ne):
            rows_e = jnp.int32(0)
            for s in range(n_sub):
                meta_ref[seg_xg + s * ne + e] = run + rows_e
                rows_e = rows_e + meta_ref[seg_len + s * ne + e]
            blocks = (rows_e + MOE_ROWS - 1) // MOE_ROWS
            meta_ref[e_base + e] = run
            meta_ref[e_blk + e] = blocks
            _zero_rows(xg_ref, run + rows_e, (blocks * MOE_ROWS - rows_e) // MOE_ALIGN)
            run = run + blocks * MOE_ROWS
        for s in range(n_sub):
            h2_s = h2_ref[0, s * s_len:(s + 1) * s_len, :]
            t1, t2 = route_ref[s, 0:1, :], route_ref[s, 1:2, :]
            for r0 in range(0, cb_rows, MOE_GATHER_ROWS):
                onehot = onehot_rows(r0, MOE_GATHER_ROWS, t1, t2, 1.0, 1.0).astype(BF16)
                cb_ref[r0:r0 + MOE_GATHER_ROWS, :] = _dot(onehot, h2_s).astype(BF16)
            for e in range(ne):
                _copy_rows(cb_ref, meta_ref[seg_cb + s * ne + e], xg_ref, meta_ref[seg_xg + s * ne + e],
                           meta_ref[seg_len + s * ne + e] // MOE_ALIGN)

    @pl.when(step < ne)
    def _expert():
        base = meta_ref[e_base + step]
        n_blk = meta_ref[e_blk + step]

        def run_rows(r0, rows):
            xe = xg_ref[pl.ds(r0, rows), :]
            hid = (_silu(_dot(xe, wg_ref[0])) * _dot(xe, wu_ref[0])).astype(BF16)
            xg_ref[pl.ds(r0, rows), :] = _dot(hid, wd_ref[0]).astype(BF16)

        def pair(j, carry):
            run_rows(pl.multiple_of(base + j * 2 * MOE_ROWS, MOE_ROWS), 2 * MOE_ROWS)
            return carry

        lax.fori_loop(0, n_blk // 2, pair, 0)

        @pl.when(n_blk % 2 == 1)
        def _tail():
            run_rows(pl.multiple_of(base + (n_blk - 1) * MOE_ROWS, MOE_ROWS), MOE_ROWS)

    @pl.when(step >= ne)
    def _combine():
        s = step - ne
        for e in range(ne):
            _copy_rows(xg_ref, meta_ref[seg_xg + s * ne + e], cb_ref, meta_ref[seg_cb + s * ne + e],
                       meta_ref[seg_len + s * ne + e] // MOE_ALIGN)
        used = meta_ref[sub_rows + s]
        _zero_rows(cb_ref, used, (cb_rows - used) // MOE_ALIGN)
        t1, t2 = route_ref[s, 0:1, :], route_ref[s, 1:2, :]
        g1, g2 = route_ref[s, 2:3, :], route_ref[s, 3:4, :]
        acc = jnp.zeros((s_len, D_MODEL), F32)
        for r0 in range(0, cb_rows, MOE_GATHER_ROWS):
            weighted = onehot_rows(r0, MOE_GATHER_ROWS, t1, t2, g1, g2).astype(BF16)
            acc = acc + _dot_tn(weighted, cb_ref[r0:r0 + MOE_GATHER_ROWS, :])
        v = DEEPNORM_ALPHA * x1_ref[0] + g2_ref[0] * acc
        out_ref[0] = _layer_norm(v, lng_ref[...], lnb_ref[...])


def _moe(h2, x1, g2, wr_t, br, tri, wg, wu, wd, lng, lnb, tm, s_len):
    bsz, seqlen, _ = x1.shape
    n_sub = tm // s_len
    ne = N_EXPERTS
    cb_rows = _round_up(2 * s_len + ne * (MOE_ALIGN - 1), MOE_GATHER_ROWS)
    xg_rows = _round_up(2 * tm + n_sub * ne * (MOE_ALIGN - 1) + ne * (MOE_ROWS - 1), MOE_ROWS)
    tok = lambda b, i, st: (b, i, 0)
    sub = lambda b, i, st: (b, i * n_sub + jnp.clip(st - ne, 0, n_sub - 1), 0)
    per_b = lambda b, i, st: (b, 0, 0)
    const = lambda b, i, st: (0, 0)
    per_e = lambda b, i, st: (jnp.minimum(st, ne - 1), 0, 0)
    row = lambda v: v.astype(F32).reshape(1, -1)
    return pl.pallas_call(
        _moe_kernel,
        grid=(bsz, seqlen // tm, ne + n_sub),
        in_specs=[
            pl.BlockSpec((1, tm, D_MODEL), tok),
            pl.BlockSpec((1, s_len, D_MODEL), sub),
            pl.BlockSpec((1, 1, D_MODEL), per_b),
            pl.BlockSpec((ne, D_MODEL), const),
            pl.BlockSpec((ne, 1), const),
            pl.BlockSpec((s_len, s_len), const),
            pl.BlockSpec((1, D_MODEL, D_EXPERT), per_e),
            pl.BlockSpec((1, D_MODEL, D_EXPERT), per_e),
            pl.BlockSpec((1, D_EXPERT, D_MODEL), per_e),
            pl.BlockSpec((1, D_MODEL), const),
            pl.BlockSpec((1, D_MODEL), const),
        ],
        out_specs=pl.BlockSpec((1, s_len, D_MODEL), sub),
        out_shape=jax.ShapeDtypeStruct((bsz, seqlen, D_MODEL), F32),
        scratch_shapes=[pltpu.VMEM((xg_rows, D_MODEL), BF16), pltpu.VMEM((cb_rows, D_MODEL), BF16),
                        pltpu.VMEM((n_sub, 8, s_len), F32), pltpu.SMEM((256,), jnp.int32)],
        compiler_params=_params("parallel", "parallel", "arbitrary"),
    )(h2, x1, g2, wr_t, br, tri, wg, wu, wd, row(lng), row(lnb))


def _layer_weights(i, w):
    w_in = w['w_in'][i]
    n_dt = 2 * SSD_HEADS
    dt_lo = SSD_WIDTH + SSD_CONV_CH
    w_dt = w_in[:, dt_lo:dt_lo + n_dt]
    w_u = w_in[:, dt_lo + n_dt:]
    w_all = jnp.concatenate(
        [w_in[:, :dt_lo], w_dt, jnp.zeros((D_MODEL, DT_PAD - n_dt), w_in.dtype)], axis=1).astype(BF16)
    w_t = jnp.concatenate([w_u, w_dt], axis=1).T.astype(BF16)
    m, f, e, tab = _s5_tables(w['s5_lam_re'][i], w['s5_lam_im'][i], w['s5_log_step'][i],
                              w['s5_b_re'][i], w['s5_b_im'][i], w['s5_c_re'][i], w['s5_c_im'][i])
    return dict(
        w_all=w_all, w_t=w_t,
        s5=(m, f, e, tab),
        wg=w['w_gate'][i].astype(BF16), wu=w['w_up'][i].astype(BF16), wd=w['w_down'][i].astype(BF16),
    )


def _trunk(x, mods, w, layer_w, tiles):
    bsz, seqlen, _ = x.shape
    tm, tq, tmoe = tiles
    wr_t = w['w_router'].T.astype(BF16)
    br = w['b_router'].astype(F32).reshape(N_EXPERTS, 1)
    s_len = min(MOE_SUB, tmoe)
    ids = jnp.arange(s_len)
    tri = (ids[:, None] < ids[None, :]).astype(BF16)
    for i in range(DEPTH):
        lw = layer_w[i]
        sh1, sc1, g1, sh2, sc2, g2 = [v.reshape(bsz, 1, D_MODEL) for v in jnp.split(mods[i], 6, axis=-1)]
        z, xbc, dt, u_t, dt_t = _inproj(x, sc1, sh1, lw['w_all'], lw['w_t'], tm)
        xbc = _conv_silu(xbc, w['conv_w'][i], w['conv_b'][i], tm)
        y_f = _ssd_dir(xbc, dt, dt_t, w['ssd_a_log'][i], w['ssd_dt_bias'][i], False, tq)
        y_ssd = _ssd_dir(xbc, dt, dt_t, w['ssd_a_log'][i], w['ssd_dt_bias'][i], True, tq,
                         z=z, yf=y_f, d_skip=w['ssd_d'][i], norm_w=w['ssd_norm_w'][i])
        y_s5_t = _s5_scan(u_t, *lw['s5'])
        x1, h2 = _outproj(y_ssd, y_s5_t, u_t, x, g1, sc2, sh2, w['s5_d'][i], w['s5_w_glu'][i],
                          w['s5_b_glu'][i], w['w_out'][i], w['ln1_g'][i], w['ln1_b'][i], tm)
        x = _moe(h2, x1, g2, wr_t, br, tri, lw['wg'], lw['wu'], lw['wd'],
                 w['ln2_g'][i], w['ln2_b'][i], tmoe, s_len)
    return x


def _pick(n, pref):
    return pref if n % pref == 0 else n


def _run(x_groups, c_groups, w):
    rows = sum(c.shape[0] for c in c_groups)
    pad = (-rows) % 8
    c_all = jnp.concatenate(list(c_groups) + [jnp.zeros((pad, D_MODEL), F32)], axis=0).astype(F32)
    mods = _ada_mod(c_all, w['w_ada'].astype(F32), w['b_ada'].astype(F32))
    layer_w = [_layer_weights(i, w) for i in range(DEPTH)]
    outs = []
    r0 = 0
    for x, c in zip(x_groups, c_groups):
        bsz, seqlen, _ = x.shape
        assert seqlen % (8 * S5_T) == 0, "sequence length must be a multiple of 8 S5 chunks"
        tiles = (_pick(seqlen, 512), _pick(seqlen, 512), _pick(seqlen, 2048))
        outs.append(_trunk(x, mods[:, r0:r0 + bsz], w, layer_w, tiles))
        r0 += bsz
    return tuple(outs)


def kernel(x_prompt, x_sample, c_prompt, c_sample, w_ada, b_ada, w_in, conv_w, conv_b, ssd_a_log, ssd_dt_bias, ssd_d, ssd_norm_w, s5_lam_re, s5_lam_im, s5_log_step, s5_b_re, s5_b_im, s5_c_re, s5_c_im, s5_d, s5_w_glu, s5_b_glu, w_out, ln1_g, ln1_b, ln2_g, ln2_b, w_router, b_router, w_gate, w_up, w_down):
    w = dict(w_ada=w_ada, b_ada=b_ada, w_in=w_in, conv_w=conv_w, conv_b=conv_b, ssd_a_log=ssd_a_log,
             ssd_dt_bias=ssd_dt_bias, ssd_d=ssd_d, ssd_norm_w=ssd_norm_w, s5_lam_re=s5_lam_re,
             s5_lam_im=s5_lam_im, s5_log_step=s5_log_step, s5_b_re=s5_b_re, s5_b_im=s5_b_im,
             s5_c_re=s5_c_re, s5_c_im=s5_c_im, s5_d=s5_d, s5_w_glu=s5_w_glu, s5_b_glu=s5_b_glu,
             w_out=w_out, ln1_g=ln1_g, ln1_b=ln1_b, ln2_g=ln2_g, ln2_b=ln2_b, w_router=w_router,
             b_router=b_router, w_gate=w_gate, w_up=w_up, w_down=w_down)
    return _run((x_prompt, x_sample), (c_prompt, c_sample), w)
```

```python
import functools
import math

import jax
import jax.numpy as jnp
from jax import lax
from jax.experimental import pallas as pl
from jax.experimental.pallas import tpu as pltpu

F32 = jnp.float32
BF16 = jnp.bfloat16

D_MODEL = 1024
DEPTH = 4
SSD_WIDTH = 512
SSD_HEADDIM = 64
SSD_HEADS = 8
SSD_GROUPS = 2
SSD_STATE = 128
SSD_CONV = 5
SSD_CONV_CH = SSD_WIDTH + 2 * SSD_GROUPS * SSD_STATE
S5_WIDTH = 512
S5_GROUP_CH = 16
S5_GROUPS = 32
S5_STATE = 64
S5_MIN_DECAY = 1e-4
N_EXPERTS = 16
EXPERTS_PER_GROUP = 4
D_EXPERT = 512
DEEPNORM_ALPHA = (2 * DEPTH) ** 0.25
LN_EPS = 1e-5
RMS_EPS = 1e-5

SSD_Q = 128
S5_T = 128
PROJ_COLS = SSD_WIDTH + SSD_CONV_CH
DT_PAD = 128
CONV_HALO = 16
MOE_ROWS = 128
MOE_ALIGN = 16
MOE_GATHER_ROWS = 256
MOE_SUB = 512
VMEM_LIMIT = 56 * 1024 * 1024


def _dot(a, b):
    return jnp.dot(a, b, preferred_element_type=F32)


def _dot_nt(a, b):
    return lax.dot_general(a, b, (((1,), (1,)), ((), ())), preferred_element_type=F32)


def _dot_tn(a, b):
    return lax.dot_general(a, b, (((0,), (0,)), ((), ())), preferred_element_type=F32)


def _split_bf16(v):
    hi = v.astype(BF16)
    lo = (v - hi.astype(F32)).astype(BF16)
    return hi, lo


def _silu(v):
    return v * jax.nn.sigmoid(v)


def _layer_norm(v, g, b):
    mu = jnp.mean(v, axis=-1, keepdims=True)
    vc = v - mu
    var = jnp.mean(vc * vc, axis=-1, keepdims=True)
    return vc * lax.rsqrt(var + LN_EPS) * g + b


def _params(*sem):
    return pltpu.CompilerParams(dimension_semantics=sem, vmem_limit_bytes=VMEM_LIMIT)


def _ada_kernel(c_ref, w_ref, b_ref, o_ref):
    ca = _silu(c_ref[...])
    c_hi, c_lo = _split_bf16(ca)
    w_hi, w_lo = _split_bf16(w_ref[0])
    acc = _dot(c_hi, w_hi) + _dot(c_lo, w_hi) + _dot(c_hi, w_lo)
    o_ref[0] = acc + b_ref[0]


def _ada_mod(c_all, w_ada, b_ada):
    rows = c_all.shape[0]
    n_out = w_ada.shape[-1]
    bn = 1536
    return pl.pallas_call(
        _ada_kernel,
        grid=(DEPTH, n_out // bn),
        in_specs=[
            pl.BlockSpec((rows, D_MODEL), lambda l, j: (0, 0)),
            pl.BlockSpec((1, D_MODEL, bn), lambda l, j: (l, 0, j)),
            pl.BlockSpec((1, 1, bn), lambda l, j: (l, 0, j)),
        ],
        out_specs=pl.BlockSpec((1, rows, bn), lambda l, j: (l, 0, j)),
        out_shape=jax.ShapeDtypeStruct((DEPTH, rows, n_out), F32),
        compiler_params=_params("arbitrary", "arbitrary"),
    )(c_all, w_ada, b_ada.reshape(DEPTH, 1, n_out))


def _inproj_kernel(x_ref, sc_ref, sh_ref, w_ref, w_t_ref,
                   z_ref, xbc_ref, dt_ref, u_t_ref, dt_t_ref):
    h = (x_ref[0] * (1.0 + sc_ref[0]) + sh_ref[0]).astype(BF16)
    p = _dot(h, w_ref[...])
    z_ref[0] = p[:, :SSD_WIDTH].astype(BF16)
    xbc_ref[0] = p[:, SSD_WIDTH:PROJ_COLS].astype(BF16)
    dt_ref[0] = p[:, PROJ_COLS:PROJ_COLS + 2 * SSD_HEADS]
    p_t = _dot_nt(w_t_ref[...], h)
    u_t_ref[0] = p_t[:S5_WIDTH].astype(BF16)
    dt_t_ref[0] = p_t[S5_WIDTH:]


def _inproj(x, sc, sh, w_all, w_t, tm):
    bsz, seqlen, _ = x.shape
    tok = lambda b, i: (b, i, 0)
    tok_t = lambda b, i: (b, 0, i)
    per_b = lambda b, i: (b, 0, 0)
    const = lambda b, i: (0, 0)
    return pl.pallas_call(
        _inproj_kernel,
        grid=(bsz, seqlen // tm),
        in_specs=[
            pl.BlockSpec((1, tm, D_MODEL), tok),
            pl.BlockSpec((1, 1, D_MODEL), per_b),
            pl.BlockSpec((1, 1, D_MODEL), per_b),
            pl.BlockSpec(w_all.shape, const),
            pl.BlockSpec(w_t.shape, const),
        ],
        out_specs=[
            pl.BlockSpec((1, tm, SSD_WIDTH), tok),
            pl.BlockSpec((1, tm, SSD_CONV_CH), tok),
            pl.BlockSpec((1, tm, 2 * SSD_HEADS), tok),
            pl.BlockSpec((1, S5_WIDTH, tm), tok_t),
            pl.BlockSpec((1, 2 * SSD_HEADS, tm), tok_t),
        ],
        out_shape=[
            jax.ShapeDtypeStruct((bsz, seqlen, SSD_WIDTH), BF16),
            jax.ShapeDtypeStruct((bsz, seqlen, SSD_CONV_CH), BF16),
            jax.ShapeDtypeStruct((bsz, seqlen, 2 * SSD_HEADS), F32),
            jax.ShapeDtypeStruct((bsz, S5_WIDTH, seqlen), BF16),
            jax.ShapeDtypeStruct((bsz, 2 * SSD_HEADS, seqlen), F32),
        ],
        compiler_params=_params("parallel", "parallel"),
    )(x, sc, sh, w_all, w_t)


def _conv_kernel(xm_ref, xl_ref, xr_ref, w_ref, b_ref, o_ref, ext_ref):
    i = pl.program_id(1)
    n = pl.num_programs(1)
    tc = xm_ref.shape[1]
    has_left = (i > 0).astype(F32)
    has_right = (i < n - 1).astype(F32)
    ext_ref[0:CONV_HALO, :] = xl_ref[0].astype(F32) * has_left
    ext_ref[CONV_HALO:CONV_HALO + tc, :] = xm_ref[0].astype(F32)
    ext_ref[CONV_HALO + tc:2 * CONV_HALO + tc, :] = xr_ref[0].astype(F32) * has_right
    acc = jnp.zeros((tc, SSD_CONV_CH), F32) + b_ref[...]
    for k in range(SSD_CONV):
        start = CONV_HALO - SSD_CONV // 2 + k
        acc = acc + ext_ref[start:start + tc, :] * w_ref[k:k + 1, :]
    o_ref[0] = _silu(acc).astype(BF16)


def _conv_silu(xbc, conv_w, conv_b, tc):
    bsz, seqlen, ch = xbc.shape
    per_blk = tc // CONV_HALO
    n_halo = seqlen // CONV_HALO
    w_pad = jnp.zeros((8, ch), F32).at[:SSD_CONV].set(conv_w.astype(F32))
    return pl.pallas_call(
        _conv_kernel,
        grid=(bsz, seqlen // tc),
        in_specs=[
            pl.BlockSpec((1, tc, ch), lambda b, i: (b, i, 0)),
            pl.BlockSpec((1, CONV_HALO, ch), lambda b, i: (b, jnp.maximum(i * per_blk - 1, 0), 0)),
            pl.BlockSpec((1, CONV_HALO, ch),
                         lambda b, i: (b, jnp.minimum((i + 1) * per_blk, n_halo - 1), 0)),
            pl.BlockSpec((8, ch), lambda b, i: (0, 0)),
            pl.BlockSpec((1, ch), lambda b, i: (0, 0)),
        ],
        out_specs=pl.BlockSpec((1, tc, ch), lambda b, i: (b, i, 0)),
        out_shape=jax.ShapeDtypeStruct((bsz, seqlen, ch), BF16),
        scratch_shapes=[pltpu.VMEM((tc + 2 * CONV_HALO, ch), F32)],
        compiler_params=_params("parallel", "parallel"),
    )(xbc, xbc, xbc, w_pad, conv_b.astype(F32).reshape(1, ch))


def _softplus(v):
    return jnp.maximum(v, 0.0) + jnp.log1p(jnp.exp(-jnp.abs(v)))


def _head_indicator(width):
    head = lax.broadcasted_iota(jnp.int32, (SSD_HEADS, SSD_HEADS * width), 0)
    lane = lax.broadcasted_iota(jnp.int32, (SSD_HEADS, SSD_HEADS * width), 1)
    return jnp.where(lane // width == head, 1.0, 0.0).astype(BF16)


def _ssd_chunk(xbc, dt_raw, dt_raw_t, a_row, a_col, bias_row, bias_col, state_ref, rev):
    q = xbc.shape[0]
    x = xbc[:, :SSD_WIDTH]
    b_mat = xbc[:, SSD_WIDTH:SSD_WIDTH + SSD_GROUPS * SSD_STATE]
    c_mat = xbc[:, SSD_WIDTH + SSD_GROUPS * SSD_STATE:]
    dts = _softplus(dt_raw + bias_row)
    dts_t = _softplus(dt_raw_t + bias_col)
    dta = dts * a_row
    dta_t = dts_t * a_col

    row = lax.broadcasted_iota(jnp.int32, (q, q), 0)
    col = lax.broadcasted_iota(jnp.int32, (q, q), 1)
    lower = col <= row
    upper = col >= row
    tri_lower = jnp.where(lower, 1.0, 0.0).astype(BF16)
    tri_upper = jnp.where(upper, 1.0, 0.0).astype(BF16)
    tri_c, tri_r, mask = (tri_upper, tri_lower, upper) if rev else (tri_lower, tri_upper, lower)

    d_hi, d_lo = _split_bf16(dta)
    acc = _dot(tri_c, d_hi) + _dot(tri_c, d_lo)
    t_hi, t_lo = _split_bf16(dta_t)
    acc_t = _dot(t_hi, tri_r) + _dot(t_lo, tri_r)

    lane128 = lax.broadcasted_iota(jnp.int32, (q, 128), 1)
    x32 = x.astype(F32)
    cb = [_dot_nt(c_mat[:, g * SSD_STATE:(g + 1) * SSD_STATE],
                  b_mat[:, g * SSD_STATE:(g + 1) * SSD_STATE]) for g in range(SSD_GROUPS)]
    a_hi, a_lo = _split_bf16(acc)
    ind_q = _head_indicator(q)
    t_row = jnp.concatenate([acc_t[h:h + 1, :] for h in range(SSD_HEADS)], axis=1)
    seg_all = _dot(a_hi, ind_q) + _dot(a_lo, ind_q) - t_row
    ys = []
    for k in range(SSD_HEADS // 2):
        m_pair = []
        for hh in range(2):
            h = 2 * k + hh
            g = h // (SSD_HEADS // SSD_GROUPS)
            seg = seg_all[:, h * q:(h + 1) * q]
            dec = jnp.where(mask, jnp.exp(jnp.minimum(seg, 0.0)), 0.0)
            m_pair.append((cb[g] * dec * dts_t[h:h + 1, :]).astype(BF16))
        xp = x32[:, 128 * k:128 * (k + 1)]
        x_blk = jnp.concatenate([jnp.where(lane128 < SSD_HEADDIM, xp, 0.0).astype(BF16),
                                 jnp.where(lane128 < SSD_HEADDIM, 0.0, xp).astype(BF16)], axis=0)
        ys.append(_dot(jnp.concatenate(m_pair, axis=1), x_blk))
    y = jnp.concatenate(ys, axis=1)

    end = acc[0:1, :] if rev else acc[q - 1:q, :]
    ind_p = _head_indicator(SSD_HEADDIM)
    y_scale = _dot(jnp.exp(acc).astype(BF16), ind_p)
    x_w = _dot((dts * jnp.exp(end - acc)).astype(BF16), ind_p)
    c_hi, c_lo = _split_bf16(jnp.broadcast_to(jnp.exp(end), (8, SSD_HEADS)))
    carry = (_dot(c_hi, ind_p) + _dot(c_lo, ind_p))[0:1, :]
    xw = (x32 * x_w).astype(BF16)
    half = SSD_WIDTH // SSD_GROUPS
    y_off = []
    for g in range(SSD_GROUPS):
        st = state_ref[g]
        y_off.append(_dot(c_mat[:, g * SSD_STATE:(g + 1) * SSD_STATE], st.astype(BF16)))
        state_ref[g] = st * carry[:, g * half:(g + 1) * half] + _dot_tn(
            b_mat[:, g * SSD_STATE:(g + 1) * SSD_STATE], xw[:, g * half:(g + 1) * half])
    return y + jnp.concatenate(y_off, axis=1) * y_scale


def _ssd_dir_kernel(*refs, rev):
    if rev:
        (xbc_ref, dt_ref, dt_t_ref, alog_ref, alog_t_ref, bias_ref, bias_t_ref,
         z_ref, yf_ref, dskip_ref, nw_ref, o_ref, state_ref) = refs
    else:
        (xbc_ref, dt_ref, dt_t_ref, alog_ref, alog_t_ref, bias_ref, bias_t_ref,
         o_ref, state_ref) = refs

    @pl.when(pl.program_id(1) == 0)
    def _():
        state_ref[...] = jnp.zeros(state_ref.shape, F32)

    d = 1 if rev else 0
    hs = slice(d * SSD_HEADS, (d + 1) * SSD_HEADS)
    a_row = -jnp.exp(alog_ref[:, hs])
    a_col = -jnp.exp(alog_t_ref[hs, :])
    bias_row = bias_ref[:, hs]
    bias_col = bias_t_ref[hs, :]
    tq = xbc_ref.shape[1]
    n_chunks = tq // SSD_Q
    order = range(n_chunks - 1, -1, -1) if rev else range(n_chunks)
    for ci in order:
        rs = slice(ci * SSD_Q, (ci + 1) * SSD_Q)
        xbc = xbc_ref[0, rs, :]
        y = _ssd_chunk(xbc, dt_ref[0, rs, hs], dt_t_ref[0, hs, rs], a_row, a_col,
                       bias_row, bias_col, state_ref, rev)
        if rev:
            y = y + yf_ref[0, rs, :].astype(F32) + xbc[:, :SSD_WIDTH].astype(F32) * dskip_ref[...]
            y = y * _silu(z_ref[0, rs, :].astype(F32))
            y = y * lax.rsqrt(jnp.mean(y * y, axis=-1, keepdims=True) + RMS_EPS) * nw_ref[...]
            o_ref[0, rs, :] = y.astype(BF16)
        else:
            o_ref[0, rs, :] = y.astype(BF16)


def _ssd_dir(xbc, dt, dt_t, a_log, dt_bias, rev, tq, z=None, yf=None, d_skip=None, norm_w=None):
    bsz, seqlen, _ = xbc.shape
    nblk = seqlen // tq
    if rev:
        tok = lambda b, i: (b, nblk - 1 - i, 0)
        tok_t = lambda b, i: (b, 0, nblk - 1 - i)
    else:
        tok = lambda b, i: (b, i, 0)
        tok_t = lambda b, i: (b, 0, i)
    const = lambda b, i: (0, 0)
    nh2 = 2 * SSD_HEADS
    a_flat = a_log.astype(F32).reshape(1, nh2)
    bias_flat = dt_bias.astype(F32).reshape(1, nh2)
    args = [xbc, dt, dt_t, a_flat, a_flat.reshape(nh2, 1), bias_flat, bias_flat.reshape(nh2, 1)]
    in_specs = [
        pl.BlockSpec((1, tq, SSD_CONV_CH), tok),
        pl.BlockSpec((1, tq, nh2), tok),
        pl.BlockSpec((1, nh2, tq), tok_t),
        pl.BlockSpec((1, nh2), const),
        pl.BlockSpec((nh2, 1), const),
        pl.BlockSpec((1, nh2), const),
        pl.BlockSpec((nh2, 1), const),
    ]
    if rev:
        args += [z, yf, jnp.repeat(d_skip.astype(F32), SSD_HEADDIM).reshape(1, SSD_WIDTH),
                 norm_w.astype(F32).reshape(1, SSD_WIDTH)]
        in_specs += [
            pl.BlockSpec((1, tq, SSD_WIDTH), tok),
            pl.BlockSpec((1, tq, SSD_WIDTH), tok),
            pl.BlockSpec((1, SSD_WIDTH), const),
            pl.BlockSpec((1, SSD_WIDTH), const),
        ]
    return pl.pallas_call(
        functools.partial(_ssd_dir_kernel, rev=rev),
        grid=(bsz, nblk),
        in_specs=in_specs,
        out_specs=pl.BlockSpec((1, tq, SSD_WIDTH), tok),
        out_shape=jax.ShapeDtypeStruct((bsz, seqlen, SSD_WIDTH), BF16),
        scratch_shapes=[pltpu.VMEM((SSD_GROUPS, SSD_STATE, SSD_WIDTH // SSD_GROUPS), F32)],
        compiler_params=_params("parallel", "arbitrary"),
    )(*args)


def _s5_toeplitz_kernel(k_ref, m_ref):
    t = S5_T
    gc = S5_GROUP_CH
    def rows_of_channel(i, carry):
        r0 = pl.multiple_of(i * t, t)
        for o in range(gc):
            taps = jnp.broadcast_to(k_ref[0, i, o:o + 1, :], (t, 2 * t))
            skew = pltpu.roll(taps, t + 1, 1, stride=1, stride_axis=0)
            m_ref[0, pl.ds(r0, t), o * t:(o + 1) * t] = skew[:, :t].astype(BF16)
        return carry

    lax.fori_loop(0, gc, rows_of_channel, 0)


def _s5_toeplitz(kall):
    g, gc, _, taps = kall.shape
    n = gc * S5_T
    return pl.pallas_call(
        _s5_toeplitz_kernel,
        grid=(g,),
        in_specs=[pl.BlockSpec((1, gc, gc, taps), lambda q: (q, 0, 0, 0))],
        out_specs=pl.BlockSpec((1, n, n), lambda q: (q, 0, 0)),
        out_shape=jax.ShapeDtypeStruct((g, n, n), BF16),
        compiler_params=_params("parallel"),
    )(kall)


def _s5_tables(lam_re, lam_im, log_step, b_re, b_im, c_re, c_im):
    hp = lax.Precision.HIGHEST
    t = S5_T
    g, p, gc = S5_GROUPS, S5_STATE, S5_GROUP_CH
    lr = jnp.minimum(lam_re.astype(F32), -S5_MIN_DECAY)
    li = lam_im.astype(F32)
    step = jnp.exp(log_step.astype(F32))[..., None]
    ar, ai = lr * step, li * step

    def power(n):
        mag = jnp.exp(n * ar)
        return mag * jnp.cos(n * ai), mag * jnp.sin(n * ai)

    def power_per_step(n):
        arg_r = n[:, None] * ar[:, :, None, :]
        arg_i = n[:, None] * ai[:, :, None, :]
        mag = jnp.exp(arg_r)
        return mag * jnp.cos(arg_i), mag * jnp.sin(arg_i)

    lbr, lbi = power(1.0)
    den = lr * lr + li * li
    qr = ((lbr - 1.0) * lr + lbi * li) / den
    qi = (lbi * lr - (lbr - 1.0) * li) / den
    bre, bim = b_re.astype(F32)[None], b_im.astype(F32)[None]
    bbr = qr[..., None] * bre - qi[..., None] * bim
    bbi = qr[..., None] * bim + qi[..., None] * bre
    cr, ci = c_re.astype(F32), c_im.astype(F32)

    steps = jnp.arange(t, dtype=F32)
    pwr, pwi = power(steps[:, None, None, None])
    bbr_t, bbi_t = jnp.swapaxes(bbr, -1, -2), jnp.swapaxes(bbi, -1, -2)
    prod_re = cr[:, :, :, None, :] * bbr_t[:, :, None, :, :] - ci[:, :, :, None, :] * bbi_t[:, :, None, :, :]
    prod_im = cr[:, :, :, None, :] * bbi_t[:, :, None, :, :] + ci[:, :, :, None, :] * bbr_t[:, :, None, :, :]
    kern = (jnp.einsum('kdgp,dgoip->kdgoi', pwr, prod_re, precision=hp)
            - jnp.einsum('kdgp,dgoip->kdgoi', pwi, prod_im, precision=hp))
    kf, kb = kern[:, 0], kern[:, 1]
    kall = jnp.concatenate([kb[:0:-1], (kf[0] + kb[0])[None], kf[1:], jnp.zeros_like(kf[:1])], axis=0)
    m = _s5_toeplitz(jnp.transpose(kall, (1, 3, 2, 0)))

    def lane_pad(v):
        return jnp.pad(v, [(0, 0)] * (v.ndim - 1) + [(0, p)])

    def state_maps(exponents, d):
        pr, pi = power_per_step(exponents)
        pr, pi = pr[d][:, None], pi[d][:, None]
        br, bi = bbr_t[d][:, :, None, :], bbi_t[d][:, :, None, :]
        rows = lambda v: lane_pad(v).reshape(g, gc * t, 2 * p).astype(BF16)
        return rows(pr * br - pi * bi), rows(pr * bi + pi * br)

    f_slots = state_maps(t - 1.0 - steps, 0) + state_maps(steps, 1)

    def out_maps(exponents, d):
        pr, pi = power_per_step(exponents)
        pr = jnp.swapaxes(pr[d], -1, -2)[:, None]
        pi = jnp.swapaxes(pi[d], -1, -2)[:, None]
        c_r, c_i = cr[d][..., None], ci[d][..., None]
        cols = lambda v: jnp.pad(v, [(0, 0), (0, 0), (0, p), (0, 0)]).astype(BF16)
        return cols(c_r * pr - c_i * pi), cols(-(c_r * pi + c_i * pr))

    e_slots = out_maps(steps + 1.0, 0) + out_maps(t - steps, 1)
    pad = lambda v, axis: jnp.concatenate([v, jnp.zeros_like(v)], axis=axis)

    step_r, step_i = power(jnp.array([1.0, 2.0, 4.0, 8.0], F32)[:, None, None, None] * t)
    rows_a = []
    for d in range(2):
        for s in range(4):
            rows_a += [step_r[s, d], step_i[s, d]]
    r8 = jnp.arange(8, dtype=F32)[:, None, None, None] * t
    car_r, car_i = power(r8)
    rows = jnp.concatenate([jnp.stack(rows_a, axis=0), car_r[:, 0], car_i[:, 0],
                            car_r[::-1, 1], car_i[::-1, 1]], axis=0)
    tab = pad(jnp.transpose(rows, (1, 0, 2)), 2)
    return m, f_slots, e_slots, tab


def _s5_kernel(u_ref, m_ref, f0_ref, f1_ref, f2_ref, f3_ref, e0_ref, e1_ref, e2_ref, e3_ref,
               tab_ref, y_ref, s_ref, xin_ref):
    bsz, gc, n_c, t = u_ref.shape
    lanes = 2 * S5_STATE
    u = jnp.concatenate(
        [jnp.concatenate([u_ref[b, i] for i in range(gc)], axis=1) for b in range(bsz)], axis=0)
    f_all = jnp.concatenate([r[0] for r in (f0_ref, f1_ref, f2_ref, f3_ref)], axis=1)
    e_all = jnp.concatenate(
        [jnp.concatenate([r[0, o] for o in range(gc)], axis=1)
         for r in (e0_ref, e1_ref, e2_ref, e3_ref)], axis=0)
    s_ref[...] = _dot(u, f_all)

    row = lax.broadcasted_iota(jnp.int32, (8, lanes), 0)
    tab_a = tab_ref[0, 0:16, :]
    bc = lambda v: jnp.broadcast_to(v, (8, lanes))
    step_w = [[(bc(tab_a[8 * d + 2 * s:8 * d + 2 * s + 1]), bc(tab_a[8 * d + 2 * s + 1:8 * d + 2 * s + 2]))
               for s in range(4)] for d in range(2)]
    car = [(tab_ref[0, 16 + 16 * d:24 + 16 * d, :], tab_ref[0, 24 + 16 * d:32 + 16 * d, :])
           for d in range(2)]

    def shifted(v, sh, d):
        if d == 0:
            return jnp.where(row >= sh, pltpu.roll(v, sh, 0), 0.0)
        return jnp.where(row < 8 - sh, pltpu.roll(v, 8 - sh, 0), 0.0)

    def tile_scan(r0, hr, hi, d):
        c0 = 2 * lanes * d
        zr = s_ref[r0:r0 + 8, c0:c0 + lanes]
        zi = s_ref[r0:r0 + 8, c0 + lanes:c0 + 2 * lanes]
        for s, sh in enumerate((1, 2, 4)):
            wr, wi = step_w[d][s]
            sr, si = shifted(zr, sh, d), shifted(zi, sh, d)
            zr, zi = zr + wr * sr - wi * si, zi + wr * si + wi * sr
        pr, pi = car[d]
        xin_ref[r0:r0 + 8, c0:c0 + lanes] = pr * hr - pi * hi + shifted(zr, 1, d)
        xin_ref[r0:r0 + 8, c0 + lanes:c0 + 2 * lanes] = pr * hi + pi * hr + shifted(zi, 1, d)
        last = 0 if d else 7
        wr, wi = step_w[d][3]
        return (wr * hr - wi * hi + bc(zr[last:last + 1]), wr * hi + wi * hr + bc(zi[last:last + 1]))

    n_t = n_c // 8
    zero = jnp.zeros((8, lanes), F32)
    for b in range(bsz):
        fr, fi, br, bi = zero, zero, zero, zero
        for k in range(n_t):
            fr, fi = tile_scan(b * n_c + 8 * k, fr, fi, 0)
            br, bi = tile_scan(b * n_c + 8 * (n_t - 1 - k), br, bi, 1)

    y = _dot(u, m_ref[0]) + _dot(xin_ref[...].astype(BF16), e_all)
    for b in range(bsz):
        for o in range(gc):
            y_ref[b, o] = y[b * n_c:(b + 1) * n_c, o * t:(o + 1) * t].astype(BF16)


def _s5_scan(u_t, m, f_slots, e_slots, tab):
    bsz, width, seqlen = u_t.shape
    n_c = seqlen // S5_T
    gc = S5_GROUP_CH
    rows = bsz * n_c
    f_spec = pl.BlockSpec((1, gc * S5_T, 2 * S5_STATE), lambda q: (q, 0, 0))
    e_spec = pl.BlockSpec((1, gc, 2 * S5_STATE, S5_T), lambda q: (q, 0, 0, 0))
    y = pl.pallas_call(
        _s5_kernel,
        grid=(S5_GROUPS,),
        in_specs=[
            pl.BlockSpec((bsz, gc, n_c, S5_T), lambda q: (0, q, 0, 0)),
            pl.BlockSpec((1, gc * S5_T, gc * S5_T), lambda q: (q, 0, 0)),
            f_spec, f_spec, f_spec, f_spec, e_spec, e_spec, e_spec, e_spec,
            pl.BlockSpec((1, 48, 2 * S5_STATE), lambda q: (q, 0, 0)),
        ],
        out_specs=pl.BlockSpec((bsz, gc, n_c, S5_T), lambda q: (0, q, 0, 0)),
        out_shape=jax.ShapeDtypeStruct((bsz, width, n_c, S5_T), BF16),
        scratch_shapes=[pltpu.VMEM((rows, 8 * S5_STATE), F32), pltpu.VMEM((rows, 8 * S5_STATE), F32)],
        compiler_params=_params("parallel"),
    )(u_t.reshape(bsz, width, n_c, S5_T), m, *f_slots, *e_slots, tab)
    return y.reshape(bsz, width, seqlen)


def _gelu_tanh(v):
    return 0.5 * v * (1.0 + jnp.tanh(math.sqrt(2.0 / math.pi) * (v + 0.044715 * (v * v * v))))


def _outproj_kernel(yssd_ref, ys5_ref, u_ref, x_ref, g1_ref, sc2_ref, sh2_ref, d5_ref,
                    wglu_t_ref, bglu_ref, wout_ref, lng_ref, lnb_ref, x1_ref, h2_ref):
    y = ys5_ref[0].astype(F32) + u_ref[0].astype(F32) * d5_ref[...]
    y = _gelu_tanh(y)
    s5 = y * jax.nn.sigmoid(_dot(wglu_t_ref[...], y.astype(BF16)) + bglu_ref[...])
    mix = (_dot(yssd_ref[0], wout_ref[:SSD_WIDTH, :])
           + _dot_tn(s5.astype(BF16), wout_ref[SSD_WIDTH:, :]))
    x1 = _layer_norm(DEEPNORM_ALPHA * x_ref[0] + g1_ref[0] * mix, lng_ref[...], lnb_ref[...])
    x1_ref[0] = x1
    h2_ref[0] = (x1 * (1.0 + sc2_ref[0]) + sh2_ref[0]).astype(BF16)


def _outproj(yssd, ys5_t, u_t, x, g1, sc2, sh2, d5, wglu, bglu, wout, lng, lnb, tm):
    bsz, seqlen, _ = x.shape
    tok = lambda b, i: (b, i, 0)
    tok_t = lambda b, i: (b, 0, i)
    per_b = lambda b, i: (b, 0, 0)
    const = lambda b, i: (0, 0)
    row = lambda v: v.astype(F32).reshape(1, -1)
    col = lambda v: v.astype(F32).reshape(-1, 1)
    return pl.pallas_call(
        _outproj_kernel,
        grid=(bsz, seqlen // tm),
        in_specs=[
            pl.BlockSpec((1, tm, SSD_WIDTH), tok),
            pl.BlockSpec((1, S5_WIDTH, tm), tok_t),
            pl.BlockSpec((1, S5_WIDTH, tm), tok_t),
            pl.BlockSpec((1, tm, D_MODEL), tok),
            pl.BlockSpec((1, 1, D_MODEL), per_b),
            pl.BlockSpec((1, 1, D_MODEL), per_b),
            pl.BlockSpec((1, 1, D_MODEL), per_b),
            pl.BlockSpec((S5_WIDTH, 1), const),
            pl.BlockSpec((S5_WIDTH, S5_WIDTH), const),
            pl.BlockSpec((S5_WIDTH, 1), const),
            pl.BlockSpec((D_MODEL, D_MODEL), const),
            pl.BlockSpec((1, D_MODEL), const),
            pl.BlockSpec((1, D_MODEL), const),
        ],
        out_specs=[pl.BlockSpec((1, tm, D_MODEL), tok), pl.BlockSpec((1, tm, D_MODEL), tok)],
        out_shape=[jax.ShapeDtypeStruct((bsz, seqlen, D_MODEL), F32),
                   jax.ShapeDtypeStruct((bsz, seqlen, D_MODEL), BF16)],
        compiler_params=_params("parallel", "parallel"),
    )(yssd, ys5_t, u_t, x, g1, sc2, sh2, col(d5), wglu.T.astype(BF16), col(bglu), wout.astype(BF16),
      row(lng), row(lnb))


def _round_up(v, m):
    return (v + m - 1) // m * m


def _route_subtile(h2_s, wr_t, br, tri):
    s_len = h2_s.shape[0]
    logits = _dot_nt(wr_t, h2_s) + br
    mx = jnp.max(logits, axis=0, keepdims=True)
    ex = jnp.exp(logits - mx)
    probs = ex / jnp.sum(ex, axis=0, keepdims=True)
    row = lax.broadcasted_iota(jnp.int32, (N_EXPERTS, s_len), 0)
    n_grp = N_EXPERTS // EXPERTS_PER_GROUP
    gs = [jnp.max(probs[EXPERTS_PER_GROUP * g:EXPERTS_PER_GROUP * (g + 1)], axis=0, keepdims=True)
          for g in range(n_grp)]
    gmax = functools.reduce(jnp.maximum, gs)
    sel_grp = jnp.full((1, s_len), n_grp - 1, jnp.int32)
    for g in range(n_grp - 2, -1, -1):
        sel_grp = jnp.where(gs[g] == gmax, g, sel_grp)
    in_grp = (row // EXPERTS_PER_GROUP) == sel_grp
    masked = jnp.where(in_grp, probs, -1.0)
    m1 = jnp.max(masked, axis=0, keepdims=True)
    i1 = jnp.min(jnp.where(masked == m1, row, N_EXPERTS), axis=0, keepdims=True)
    is1 = row == i1
    masked2 = jnp.where(is1, -2.0, masked)
    m2 = jnp.max(masked2, axis=0, keepdims=True)
    i2 = jnp.min(jnp.where(masked2 == m2, row, N_EXPERTS), axis=0, keepdims=True)
    is2 = row == i2
    den = m1 + m2
    sel = jnp.where(is1, 1.0, jnp.where(is2, 1.0, 0.0))
    pos = _dot(sel.astype(BF16), tri)
    return is1, is2, m1 / den, m2 / den, sel, pos


def _copy_rows(src_ref, src0, dst_ref, dst0, n_chunks):
    def body(c, carry):
        so = pl.multiple_of(src0 + c * MOE_ALIGN, MOE_ALIGN)
        do = pl.multiple_of(dst0 + c * MOE_ALIGN, MOE_ALIGN)
        dst_ref[pl.ds(do, MOE_ALIGN), :] = src_ref[pl.ds(so, MOE_ALIGN), :]
        return carry
    lax.fori_loop(0, n_chunks, body, 0)


def _zero_rows(dst_ref, dst0, n_chunks):
    def body(c, carry):
        do = pl.multiple_of(dst0 + c * MOE_ALIGN, MOE_ALIGN)
        dst_ref[pl.ds(do, MOE_ALIGN), :] = jnp.zeros((MOE_ALIGN, dst_ref.shape[1]), dst_ref.dtype)
        return carry
    lax.fori_loop(0, n_chunks, body, 0)


def _moe_kernel(h2_ref, x1_ref, g2_ref, wr_t_ref, br_ref, tri_ref, wg_ref, wu_ref, wd_ref,
                lng_ref, lnb_ref, out_ref, xg_ref, cb_ref, route_ref, meta_ref):
    step = pl.program_id(2)
    tm = h2_ref.shape[1]
    s_len = x1_ref.shape[1]
    n_sub = tm // s_len
    cb_rows = cb_ref.shape[0]
    ne = N_EXPERTS
    seg_len, seg_cb, seg_xg, e_base, e_blk, sub_rows = 0, 64, 128, 192, 208, 224

    def onehot_rows(r0, n_rows, t1, t2, v1, v2):
        slot = (lax.broadcasted_iota(jnp.int32, (n_rows, s_len), 0) + r0).astype(F32)
        return jnp.where(slot == t1, v1, jnp.where(slot == t2, v2, 0.0))

    @pl.when(step == 0)
    def _dispatch():
        erow = lax.broadcasted_iota(jnp.int32, (ne, 1), 0)
        for s in range(n_sub):
            h2_s = h2_ref[0, s * s_len:(s + 1) * s_len, :]
            is1, is2, g1, g2, sel, pos = _route_subtile(h2_s, wr_t_ref[...], br_ref[...], tri_ref[...])
            base_vec = jnp.zeros((ne, 1), F32)
            run = jnp.int32(0)
            for e in range(ne):
                n16 = _round_up(jnp.sum(sel[e:e + 1, :]).astype(jnp.int32), MOE_ALIGN)
                meta_ref[seg_len + s * ne + e] = n16
                meta_ref[seg_cb + s * ne + e] = run
                base_vec = jnp.where(erow == e, run.astype(F32), base_vec)
                run = run + n16
            meta_ref[sub_rows + s] = run
            tgt = base_vec + pos
            route_ref[s, 0:1, :] = jnp.sum(jnp.where(is1, tgt, 0.0), axis=0, keepdims=True)
            route_ref[s, 1:2, :] = jnp.sum(jnp.where(is2, tgt, 0.0), axis=0, keepdims=True)
            route_ref[s, 2:3, :] = g1
            route_ref[s, 3:4, :] = g2
        run = jnp.int32(0)
        for e in range(ne):
            rows_e = jnp.int32(0)
            for s in range(n_sub):
                meta_ref[seg_xg + s * ne + e] = run + rows_e
                rows_e = rows_e + meta_ref[seg_len + s * ne + e]
            blocks = (rows_e + MOE_ROWS - 1) // MOE_ROWS
            meta_ref[e_base + e] = run
            meta_ref[e_blk + e] = blocks
            _zero_rows(xg_ref, run + rows_e, (blocks * MOE_ROWS - rows_e) // MOE_ALIGN)
            run = run + blocks * MOE_ROWS
        for s in range(n_sub):
            h2_s = h2_ref[0, s * s_len:(s + 1) * s_len, :]
            t1, t2 = route_ref[s, 0:1, :], route_ref[s, 1:2, :]
            for r0 in range(0, cb_rows, MOE_GATHER_ROWS):
                onehot = onehot_rows(r0, MOE_GATHER_ROWS, t1, t2, 1.0, 1.0).astype(BF16)
                cb_ref[r0:r0 + MOE_GATHER_ROWS, :] = _dot(onehot, h2_s).astype(BF16)
            for e in range(ne):
                _copy_rows(cb_ref, meta_ref[seg_cb + s * ne + e], xg_ref, meta_ref[seg_xg + s * ne + e],
                           meta_ref[seg_len + s * ne + e] // MOE_ALIGN)

    @pl.when(step < ne)
    def _expert():
        base = meta_ref[e_base + step]
        n_blk = meta_ref[e_blk + step]

        def run_rows(r0, rows):
            xe = xg_ref[pl.ds(r0, rows), :]
            hid = (_silu(_dot(xe, wg_ref[0])) * _dot(xe, wu_ref[0])).astype(BF16)
            xg_ref[pl.ds(r0, rows), :] = _dot(hid, wd_ref[0]).astype(BF16)

        def pair(j, carry):
            run_rows(pl.multiple_of(base + j * 2 * MOE_ROWS, MOE_ROWS), 2 * MOE_ROWS)
            return carry

        lax.fori_loop(0, n_blk // 2, pair, 0)

        @pl.when(n_blk % 2 == 1)
        def _tail():
            run_rows(pl.multiple_of(base + (n_blk - 1) * MOE_ROWS, MOE_ROWS), MOE_ROWS)

    @pl.when(step >= ne)
    def _combine():
        s = step - ne
        for e in range(ne):
            _copy_rows(xg_ref, meta_ref[seg_xg + s * ne + e], cb_ref, meta_ref[seg_cb + s * ne + e],
                       meta_ref[seg_len + s * ne + e] // MOE_ALIGN)
        used = meta_ref[sub_rows + s]
        _zero_rows(cb_ref, used, (cb_rows - used) // MOE_ALIGN)
        t1, t2 = route_ref[s, 0:1, :], route_ref[s, 1:2, :]
        g1, g2 = route_ref[s, 2:3, :], route_ref[s, 3:4, :]
        acc = jnp.zeros((s_len, D_MODEL), F32)
        for r0 in range(0, cb_rows, MOE_GATHER_ROWS):
            weighted = onehot_rows(r0, MOE_GATHER_ROWS, t1, t2, g1, g2).astype(BF16)
            acc = acc + _dot_tn(weighted, cb_ref[r0:r0 + MOE_GATHER_ROWS, :])
        v = DEEPNORM_ALPHA * x1_ref[0] + g2_ref[0] * acc
        out_ref[0] = _layer_norm(v, lng_ref[...], lnb_ref[...])


def _moe(h2, x1, g2, wr_t, br, tri, wg, wu, wd, lng, lnb, tm, s_len):
    bsz, seqlen, _ = x1.shape
    n_sub = tm // s_len
    ne = N_EXPERTS
    cb_rows = _round_up(2 * s_len + ne * (MOE_ALIGN - 1), MOE_GATHER_ROWS)
    xg_rows = _round_up(2 * tm + n_sub * ne * (MOE_ALIGN - 1) + ne * (MOE_ROWS - 1), MOE_ROWS)
    tok = lambda b, i, st: (b, i, 0)
    sub = lambda b, i, st: (b, i * n_sub + jnp.clip(st - ne, 0, n_sub - 1), 0)
    per_b = lambda b, i, st: (b, 0, 0)
    const = lambda b, i, st: (0, 0)
    per_e = lambda b, i, st: (jnp.minimum(st, ne - 1), 0, 0)
    row = lambda v: v.astype(F32).reshape(1, -1)
    return pl.pallas_call(
        _moe_kernel,
        grid=(bsz, seqlen // tm, ne + n_sub),
        in_specs=[
            pl.BlockSpec((1, tm, D_MODEL), tok),
            pl.BlockSpec((1, s_len, D_MODEL), sub),
            pl.BlockSpec((1, 1, D_MODEL), per_b),
            pl.BlockSpec((ne, D_MODEL), const),
            pl.BlockSpec((ne, 1), const),
            pl.BlockSpec((s_len, s_len), const),
            pl.BlockSpec((1, D_MODEL, D_EXPERT), per_e),
            pl.BlockSpec((1, D_MODEL, D_EXPERT), per_e),
            pl.BlockSpec((1, D_EXPERT, D_MODEL), per_e),
            pl.BlockSpec((1, D_MODEL), const),
            pl.BlockSpec((1, D_MODEL), const),
        ],
        out_specs=pl.BlockSpec((1, s_len, D_MODEL), sub),
        out_shape=jax.ShapeDtypeStruct((bsz, seqlen, D_MODEL), F32),
        scratch_shapes=[pltpu.VMEM((xg_rows, D_MODEL), BF16), pltpu.VMEM((cb_rows, D_MODEL), BF16),
                        pltpu.VMEM((n_sub, 8, s_len), F32), pltpu.SMEM((256,), jnp.int32)],
        compiler_params=_params("parallel", "parallel", "arbitrary"),
    )(h2, x1, g2, wr_t, br, tri, wg, wu, wd, row(lng), row(lnb))


def _layer_weights(i, w):
    w_in = w['w_in'][i]
    n_dt = 2 * SSD_HEADS
    dt_lo = SSD_WIDTH + SSD_CONV_CH
    w_dt = w_in[:, dt_lo:dt_lo + n_dt]
    w_u = w_in[:, dt_lo + n_dt:]
    w_all = jnp.concatenate(
        [w_in[:, :dt_lo], w_dt, jnp.zeros((D_MODEL, DT_PAD - n_dt), w_in.dtype)], axis=1).astype(BF16)
    w_t = jnp.concatenate([w_u, w_dt], axis=1).T.astype(BF16)
    m, f, e, tab = _s5_tables(w['s5_lam_re'][i], w['s5_lam_im'][i], w['s5_log_step'][i],
                              w['s5_b_re'][i], w['s5_b_im'][i], w['s5_c_re'][i], w['s5_c_im'][i])
    return dict(
        w_all=w_all, w_t=w_t,
        s5=(m, f, e, tab),
        wg=w['w_gate'][i].astype(BF16), wu=w['w_up'][i].astype(BF16), wd=w['w_down'][i].astype(BF16),
    )


def _trunk(x, mods, w, layer_w, tiles):
    bsz, seqlen, _ = x.shape
    tm, tq, tmoe = tiles
    wr_t = w['w_router'].T.astype(BF16)
    br = w['b_router'].astype(F32).reshape(N_EXPERTS, 1)
    s_len = min(MOE_SUB, tmoe)
    ids = jnp.arange(s_len)
    tri = (ids[:, None] < ids[None, :]).astype(BF16)
    for i in range(DEPTH):
        lw = layer_w[i]
        sh1, sc1, g1, sh2, sc2, g2 = [v.reshape(bsz, 1, D_MODEL) for v in jnp.split(mods[i], 6, axis=-1)]
        z, xbc, dt, u_t, dt_t = _inproj(x, sc1, sh1, lw['w_all'], lw['w_t'], tm)
        xbc = _conv_silu(xbc, w['conv_w'][i], w['conv_b'][i], tm)
        y_f = _ssd_dir(xbc, dt, dt_t, w['ssd_a_log'][i], w['ssd_dt_bias'][i], False, tq)
        y_ssd = _ssd_dir(xbc, dt, dt_t, w['ssd_a_log'][i], w['ssd_dt_bias'][i], True, tq,
                         z=z, yf=y_f, d_skip=w['ssd_d'][i], norm_w=w['ssd_norm_w'][i])
        y_s5_t = _s5_scan(u_t, *lw['s5'])
        x1, h2 = _outproj(y_ssd, y_s5_t, u_t, x, g1, sc2, sh2, w['s5_d'][i], w['s5_w_glu'][i],
                          w['s5_b_glu'][i], w['w_out'][i], w['ln1_g'][i], w['ln1_b'][i], tm)
        x = _moe(h2, x1, g2, wr_t, br, tri, lw['wg'], lw['wu'], lw['wd'],
                 w['ln2_g'][i], w['ln2_b'][i], tmoe, s_len)
    return x


def _pick(n, pref):
    return pref if n % pref == 0 else n


def _run(x_groups, c_groups, w):
    rows = sum(c.shape[0] for c in c_groups)
    pad = (-rows) % 8
    c_all = jnp.concatenate(list(c_groups) + [jnp.zeros((pad, D_MODEL), F32)], axis=0).astype(F32)
    mods = _ada_mod(c_all, w['w_ada'].astype(F32), w['b_ada'].astype(F32))
    layer_w = [_layer_weights(i, w) for i in range(DEPTH)]
    outs = []
    r0 = 0
    for x, c in zip(x_groups, c_groups):
        bsz, seqlen, _ = x.shape
        assert seqlen % (8 * S5_T) == 0, "sequence length must be a multiple of 8 S5 chunks"
        tiles = (_pick(seqlen, 512), _pick(seqlen, 512), _pick(seqlen, 2048))
        outs.append(_trunk(x, mods[:, r0:r0 + bsz], w, layer_w, tiles))
        r0 += bsz
    return tuple(outs)


def kernel(x_prompt, x_sample, c_prompt, c_sample, w_ada, b_ada, w_in, conv_w, conv_b, ssd_a_log, ssd_dt_bias, ssd_d, ssd_norm_w, s5_lam_re, s5_lam_im, s5_log_step, s5_b_re, s5_b_im, s5_c_re, s5_c_im, s5_d, s5_w_glu, s5_b_glu, w_out, ln1_g, ln1_b, ln2_g, ln2_b, w_router, b_router, w_gate, w_up, w_down):
    w = dict(w_ada=w_ada, b_ada=b_ada, w_in=w_in, conv_w=conv_w, conv_b=conv_b, ssd_a_log=ssd_a_log,
             ssd_dt_bias=ssd_dt_bias, ssd_d=ssd_d, ssd_norm_w=ssd_norm_w, s5_lam_re=s5_lam_re,
             s5_lam_im=s5_lam_im, s5_log_step=s5_log_step, s5_b_re=s5_b_re, s5_b_im=s5_b_im,
             s5_c_re=s5_c_re, s5_c_im=s5_c_im, s5_d=s5_d, s5_w_glu=s5_w_glu, s5_b_glu=s5_b_glu,
             w_out=w_out, ln1_g=ln1_g, ln1_b=ln1_b, ln2_g=ln2_g, ln2_b=ln2_b, w_router=w_router,
             b_router=b_router, w_gate=w_gate, w_up=w_up, w_down=w_down)
    return _run((x_prompt, x_sample), (c_prompt, c_sample), w)
```

```python
import functools
import math

import jax
import jax.numpy as jnp
from jax import lax
from jax.experimental import pallas as pl
from jax.experimental.pallas import tpu as pltpu

F32 = jnp.float32
BF16 = jnp.bfloat16

D_MODEL = 1024
DEPTH = 4
SSD_WIDTH = 512
SSD_HEADDIM = 64
SSD_HEADS = 8
SSD_GROUPS = 2
SSD_STATE = 128
SSD_CONV = 5
SSD_CONV_CH = SSD_WIDTH + 2 * SSD_GROUPS * SSD_STATE
S5_WIDTH = 512
S5_GROUP_CH = 16
S5_GROUPS = 32
S5_STATE = 64
S5_MIN_DECAY = 1e-4
N_EXPERTS = 16
EXPERTS_PER_GROUP = 4
D_EXPERT = 512
DEEPNORM_ALPHA = (2 * DEPTH) ** 0.25
LN_EPS = 1e-5
RMS_EPS = 1e-5

SSD_Q = 128
S5_T = 128
PROJ_COLS = SSD_WIDTH + SSD_CONV_CH
DT_PAD = 128
CONV_HALO = 16
MOE_ROWS = 128
MOE_ALIGN = 16
MOE_GATHER_ROWS = 256
MOE_SUB = 512
MOE_EXPERTS_PER_STEP = 2
VMEM_LIMIT = 56 * 1024 * 1024


def _dot(a, b):
    return jnp.dot(a, b, preferred_element_type=F32)


def _dot_nt(a, b):
    return lax.dot_general(a, b, (((1,), (1,)), ((), ())), preferred_element_type=F32)


def _dot_tn(a, b):
    return lax.dot_general(a, b, (((0,), (0,)), ((), ())), preferred_element_type=F32)


def _split_bf16(v):
    hi = v.astype(BF16)
    lo = (v - hi.astype(F32)).astype(BF16)
    return hi, lo


def _silu(v):
    return v * jax.nn.sigmoid(v)


def _layer_norm(v, g, b):
    mu = jnp.mean(v, axis=-1, keepdims=True)
    vc = v - mu
    var = jnp.mean(vc * vc, axis=-1, keepdims=True)
    return vc * lax.rsqrt(var + LN_EPS) * g + b


def _params(*sem):
    return pltpu.CompilerParams(dimension_semantics=sem, vmem_limit_bytes=VMEM_LIMIT)


def _ada_kernel(c_ref, w_ref, b_ref, o_ref):
    ca = _silu(c_ref[...])
    c_hi, c_lo = _split_bf16(ca)
    w_hi, w_lo = _split_bf16(w_ref[0])
    acc = _dot(c_hi, w_hi) + _dot(c_lo, w_hi) + _dot(c_hi, w_lo)
    o_ref[0] = acc + b_ref[0]


def _ada_mod(c_all, w_ada, b_ada):
    rows = c_all.shape[0]
    n_out = w_ada.shape[-1]
    bn = 1536
    return pl.pallas_call(
        _ada_kernel,
        grid=(DEPTH, n_out // bn),
        in_specs=[
            pl.BlockSpec((rows, D_MODEL), lambda l, j: (0, 0)),
            pl.BlockSpec((1, D_MODEL, bn), lambda l, j: (l, 0, j)),
            pl.BlockSpec((1, 1, bn), lambda l, j: (l, 0, j)),
        ],
        out_specs=pl.BlockSpec((1, rows, bn), lambda l, j: (l, 0, j)),
        out_shape=jax.ShapeDtypeStruct((DEPTH, rows, n_out), F32),
        compiler_params=_params("arbitrary", "arbitrary"),
    )(c_all, w_ada, b_ada.reshape(DEPTH, 1, n_out))


def _inproj_kernel(x_ref, sc_ref, sh_ref, w_ref, w_t_ref,
                   z_ref, xbc_ref, dt_ref, u_t_ref, dt_t_ref):
    h = (x_ref[0] * (1.0 + sc_ref[0]) + sh_ref[0]).astype(BF16)
    p = _dot(h, w_ref[...])
    z_ref[0] = p[:, :SSD_WIDTH].astype(BF16)
    xbc_ref[0] = p[:, SSD_WIDTH:PROJ_COLS].astype(BF16)
    dt_ref[0] = p[:, PROJ_COLS:PROJ_COLS + 2 * SSD_HEADS]
    p_t = _dot_nt(w_t_ref[...], h)
    u_t_ref[0] = p_t[:S5_WIDTH].astype(BF16)
    dt_t_ref[0] = p_t[S5_WIDTH:]


def _inproj(x, sc, sh, w_all, w_t, tm):
    bsz, seqlen, _ = x.shape
    tok = lambda b, i: (b, i, 0)
    tok_t = lambda b, i: (b, 0, i)
    per_b = lambda b, i: (b, 0, 0)
    const = lambda b, i: (0, 0)
    return pl.pallas_call(
        _inproj_kernel,
        grid=(bsz, seqlen // tm),
        in_specs=[
            pl.BlockSpec((1, tm, D_MODEL), tok),
            pl.BlockSpec((1, 1, D_MODEL), per_b),
            pl.BlockSpec((1, 1, D_MODEL), per_b),
            pl.BlockSpec(w_all.shape, const),
            pl.BlockSpec(w_t.shape, const),
        ],
        out_specs=[
            pl.BlockSpec((1, tm, SSD_WIDTH), tok),
            pl.BlockSpec((1, tm, SSD_CONV_CH), tok),
            pl.BlockSpec((1, tm, 2 * SSD_HEADS), tok),
            pl.BlockSpec((1, S5_WIDTH, tm), tok_t),
            pl.BlockSpec((1, 2 * SSD_HEADS, tm), tok_t),
        ],
        out_shape=[
            jax.ShapeDtypeStruct((bsz, seqlen, SSD_WIDTH), BF16),
            jax.ShapeDtypeStruct((bsz, seqlen, SSD_CONV_CH), BF16),
            jax.ShapeDtypeStruct((bsz, seqlen, 2 * SSD_HEADS), F32),
            jax.ShapeDtypeStruct((bsz, S5_WIDTH, seqlen), BF16),
            jax.ShapeDtypeStruct((bsz, 2 * SSD_HEADS, seqlen), F32),
        ],
        compiler_params=_params("parallel", "parallel"),
    )(x, sc, sh, w_all, w_t)


def _conv_kernel(xm_ref, xl_ref, xr_ref, w_ref, b_ref, o_ref, ext_ref):
    i = pl.program_id(1)
    n = pl.num_programs(1)
    tc = xm_ref.shape[1]
    has_left = (i > 0).astype(F32)
    has_right = (i < n - 1).astype(F32)
    ext_ref[0:CONV_HALO, :] = xl_ref[0].astype(F32) * has_left
    ext_ref[CONV_HALO:CONV_HALO + tc, :] = xm_ref[0].astype(F32)
    ext_ref[CONV_HALO + tc:2 * CONV_HALO + tc, :] = xr_ref[0].astype(F32) * has_right
    acc = jnp.zeros((tc, SSD_CONV_CH), F32) + b_ref[...]
    for k in range(SSD_CONV):
        start = CONV_HALO - SSD_CONV // 2 + k
        acc = acc + ext_ref[start:start + tc, :] * w_ref[k:k + 1, :]
    o_ref[0] = _silu(acc).astype(BF16)


def _conv_silu(xbc, conv_w, conv_b, tc):
    bsz, seqlen, ch = xbc.shape
    per_blk = tc // CONV_HALO
    n_halo = seqlen // CONV_HALO
    w_pad = jnp.zeros((8, ch), F32).at[:SSD_CONV].set(conv_w.astype(F32))
    return pl.pallas_call(
        _conv_kernel,
        grid=(bsz, seqlen // tc),
        in_specs=[
            pl.BlockSpec((1, tc, ch), lambda b, i: (b, i, 0)),
            pl.BlockSpec((1, CONV_HALO, ch), lambda b, i: (b, jnp.maximum(i * per_blk - 1, 0), 0)),
            pl.BlockSpec((1, CONV_HALO, ch),
                         lambda b, i: (b, jnp.minimum((i + 1) * per_blk, n_halo - 1), 0)),
            pl.BlockSpec((8, ch), lambda b, i: (0, 0)),
            pl.BlockSpec((1, ch), lambda b, i: (0, 0)),
        ],
        out_specs=pl.BlockSpec((1, tc, ch), lambda b, i: (b, i, 0)),
        out_shape=jax.ShapeDtypeStruct((bsz, seqlen, ch), BF16),
        scratch_shapes=[pltpu.VMEM((tc + 2 * CONV_HALO, ch), F32)],
        compiler_params=_params("parallel", "parallel"),
    )(xbc, xbc, xbc, w_pad, conv_b.astype(F32).reshape(1, ch))


def _softplus(v):
    return jnp.maximum(v, 0.0) + jnp.log1p(jnp.exp(-jnp.abs(v)))


def _head_indicator(width):
    head = lax.broadcasted_iota(jnp.int32, (SSD_HEADS, SSD_HEADS * width), 0)
    lane = lax.broadcasted_iota(jnp.int32, (SSD_HEADS, SSD_HEADS * width), 1)
    return jnp.where(lane // width == head, 1.0, 0.0).astype(BF16)


def _ssd_chunk(xbc, dt_raw, dt_raw_t, a_row, a_col, bias_row, bias_col, state_ref, rev):
    q = xbc.shape[0]
    x = xbc[:, :SSD_WIDTH]
    b_mat = xbc[:, SSD_WIDTH:SSD_WIDTH + SSD_GROUPS * SSD_STATE]
    c_mat = xbc[:, SSD_WIDTH + SSD_GROUPS * SSD_STATE:]
    dts = _softplus(dt_raw + bias_row)
    dts_t = _softplus(dt_raw_t + bias_col)
    dta = dts * a_row
    dta_t = dts_t * a_col

    row = lax.broadcasted_iota(jnp.int32, (q, q), 0)
    col = lax.broadcasted_iota(jnp.int32, (q, q), 1)
    lower = col <= row
    upper = col >= row
    tri_lower = jnp.where(lower, 1.0, 0.0).astype(BF16)
    tri_upper = jnp.where(upper, 1.0, 0.0).astype(BF16)
    tri_c, tri_r, mask = (tri_upper, tri_lower, upper) if rev else (tri_lower, tri_upper, lower)

    d_hi, d_lo = _split_bf16(dta)
    acc = _dot(tri_c, d_hi) + _dot(tri_c, d_lo)
    t_hi, t_lo = _split_bf16(dta_t)
    acc_t = _dot(t_hi, tri_r) + _dot(t_lo, tri_r)

    lane128 = lax.broadcasted_iota(jnp.int32, (q, 128), 1)
    x32 = x.astype(F32)
    cb = [_dot_nt(c_mat[:, g * SSD_STATE:(g + 1) * SSD_STATE],
                  b_mat[:, g * SSD_STATE:(g + 1) * SSD_STATE]) for g in range(SSD_GROUPS)]
    a_hi, a_lo = _split_bf16(acc)
    ind_q = _head_indicator(q)
    t_row = jnp.concatenate([acc_t[h:h + 1, :] for h in range(SSD_HEADS)], axis=1)
    seg_all = _dot(a_hi, ind_q) + _dot(a_lo, ind_q) - t_row
    ys = []
    for k in range(SSD_HEADS // 2):
        m_pair = []
        for hh in range(2):
            h = 2 * k + hh
            g = h // (SSD_HEADS // SSD_GROUPS)
            seg = seg_all[:, h * q:(h + 1) * q]
            dec = jnp.where(mask, jnp.exp(jnp.minimum(seg, 0.0)), 0.0)
            m_pair.append((cb[g] * dec * dts_t[h:h + 1, :]).astype(BF16))
        xp = x32[:, 128 * k:128 * (k + 1)]
        x_blk = jnp.concatenate([jnp.where(lane128 < SSD_HEADDIM, xp, 0.0).astype(BF16),
                                 jnp.where(lane128 < SSD_HEADDIM, 0.0, xp).astype(BF16)], axis=0)
        ys.append(_dot(jnp.concatenate(m_pair, axis=1), x_blk))
    y = jnp.concatenate(ys, axis=1)

    end = acc[0:1, :] if rev else acc[q - 1:q, :]
    ind_p = _head_indicator(SSD_HEADDIM)
    y_scale = _dot(jnp.exp(acc).astype(BF16), ind_p)
    x_w = _dot((dts * jnp.exp(end - acc)).astype(BF16), ind_p)
    c_hi, c_lo = _split_bf16(jnp.broadcast_to(jnp.exp(end), (8, SSD_HEADS)))
    carry = (_dot(c_hi, ind_p) + _dot(c_lo, ind_p))[0:1, :]
    xw = (x32 * x_w).astype(BF16)
    half = SSD_WIDTH // SSD_GROUPS
    y_off = []
    for g in range(SSD_GROUPS):
        st = state_ref[g]
        y_off.append(_dot(c_mat[:, g * SSD_STATE:(g + 1) * SSD_STATE], st.astype(BF16)))
        state_ref[g] = st * carry[:, g * half:(g + 1) * half] + _dot_tn(
            b_mat[:, g * SSD_STATE:(g + 1) * SSD_STATE], xw[:, g * half:(g + 1) * half])
    return y + jnp.concatenate(y_off, axis=1) * y_scale


def _ssd_dir_kernel(*refs, rev):
    if rev:
        (xbc_ref, dt_ref, dt_t_ref, alog_ref, alog_t_ref, bias_ref, bias_t_ref,
         z_ref, yf_ref, dskip_ref, nw_ref, o_ref, state_ref) = refs
    else:
        (xbc_ref, dt_ref, dt_t_ref, alog_ref, alog_t_ref, bias_ref, bias_t_ref,
         o_ref, state_ref) = refs

    @pl.when(pl.program_id(1) == 0)
    def _():
        state_ref[...] = jnp.zeros(state_ref.shape, F32)

    d = 1 if rev else 0
    hs = slice(d * SSD_HEADS, (d + 1) * SSD_HEADS)
    a_row = -jnp.exp(alog_ref[:, hs])
    a_col = -jnp.exp(alog_t_ref[hs, :])
    bias_row = bias_ref[:, hs]
    bias_col = bias_t_ref[hs, :]
    tq = xbc_ref.shape[1]
    n_chunks = tq // SSD_Q
    order = range(n_chunks - 1, -1, -1) if rev else range(n_chunks)
    for ci in order:
        rs = slice(ci * SSD_Q, (ci + 1) * SSD_Q)
        xbc = xbc_ref[0, rs, :]
        y = _ssd_chunk(xbc, dt_ref[0, rs, hs], dt_t_ref[0, hs, rs], a_row, a_col,
                       bias_row, bias_col, state_ref, rev)
        if rev:
            y = y + yf_ref[0, rs, :].astype(F32) + xbc[:, :SSD_WIDTH].astype(F32) * dskip_ref[...]
            y = y * _silu(z_ref[0, rs, :].astype(F32))
            y = y * lax.rsqrt(jnp.mean(y * y, axis=-1, keepdims=True) + RMS_EPS) * nw_ref[...]
            o_ref[0, rs, :] = y.astype(BF16)
        else:
            o_ref[0, rs, :] = y.astype(BF16)


def _ssd_dir(xbc, dt, dt_t, a_log, dt_bias, rev, tq, z=None, yf=None, d_skip=None, norm_w=None):
    bsz, seqlen, _ = xbc.shape
    nblk = seqlen // tq
    if rev:
        tok = lambda b, i: (b, nblk - 1 - i, 0)
        tok_t = lambda b, i: (b, 0, nblk - 1 - i)
    else:
        tok = lambda b, i: (b, i, 0)
        tok_t = lambda b, i: (b, 0, i)
    const = lambda b, i: (0, 0)
    nh2 = 2 * SSD_HEADS
    a_flat = a_log.astype(F32).reshape(1, nh2)
    bias_flat = dt_bias.astype(F32).reshape(1, nh2)
    args = [xbc, dt, dt_t, a_flat, a_flat.reshape(nh2, 1), bias_flat, bias_flat.reshape(nh2, 1)]
    in_specs = [
        pl.BlockSpec((1, tq, SSD_CONV_CH), tok),
        pl.BlockSpec((1, tq, nh2), tok),
        pl.BlockSpec((1, nh2, tq), tok_t),
        pl.BlockSpec((1, nh2), const),
        pl.BlockSpec((nh2, 1), const),
        pl.BlockSpec((1, nh2), const),
        pl.BlockSpec((nh2, 1), const),
    ]
    if rev:
        args += [z, yf, jnp.repeat(d_skip.astype(F32), SSD_HEADDIM).reshape(1, SSD_WIDTH),
                 norm_w.astype(F32).reshape(1, SSD_WIDTH)]
        in_specs += [
            pl.BlockSpec((1, tq, SSD_WIDTH), tok),
            pl.BlockSpec((1, tq, SSD_WIDTH), tok),
            pl.BlockSpec((1, SSD_WIDTH), const),
            pl.BlockSpec((1, SSD_WIDTH), const),
        ]
    return pl.pallas_call(
        functools.partial(_ssd_dir_kernel, rev=rev),
        grid=(bsz, nblk),
        in_specs=in_specs,
        out_specs=pl.BlockSpec((1, tq, SSD_WIDTH), tok),
        out_shape=jax.ShapeDtypeStruct((bsz, seqlen, SSD_WIDTH), BF16),
        scratch_shapes=[pltpu.VMEM((SSD_GROUPS, SSD_STATE, SSD_WIDTH // SSD_GROUPS), F32)],
        compiler_params=_params("parallel", "arbitrary"),
    )(*args)


def _s5_toeplitz_kernel(k_ref, m_ref):
    t = S5_T
    gc = S5_GROUP_CH
    def rows_of_channel(i, carry):
        r0 = pl.multiple_of(i * t, t)
        for o in range(gc):
            taps = jnp.broadcast_to(k_ref[0, i, o:o + 1, :], (t, 2 * t))
            skew = pltpu.roll(taps, t + 1, 1, stride=1, stride_axis=0)
            m_ref[0, pl.ds(r0, t), o * t:(o + 1) * t] = skew[:, :t].astype(BF16)
        return carry

    lax.fori_loop(0, gc, rows_of_channel, 0)


def _s5_toeplitz(kall):
    g, gc, _, taps = kall.shape
    n = gc * S5_T
    return pl.pallas_call(
        _s5_toeplitz_kernel,
        grid=(g,),
        in_specs=[pl.BlockSpec((1, gc, gc, taps), lambda q: (q, 0, 0, 0))],
        out_specs=pl.BlockSpec((1, n, n), lambda q: (q, 0, 0)),
        out_shape=jax.ShapeDtypeStruct((g, n, n), BF16),
        compiler_params=_params("parallel"),
    )(kall)


def _s5_tables(lam_re, lam_im, log_step, b_re, b_im, c_re, c_im):
    hp = lax.Precision.HIGHEST
    t = S5_T
    g, p, gc = S5_GROUPS, S5_STATE, S5_GROUP_CH
    lr = jnp.minimum(lam_re.astype(F32), -S5_MIN_DECAY)
    li = lam_im.astype(F32)
    step = jnp.exp(log_step.astype(F32))[..., None]
    ar, ai = lr * step, li * step

    def power(n):
        mag = jnp.exp(n * ar)
        return mag * jnp.cos(n * ai), mag * jnp.sin(n * ai)

    def power_per_step(n):
        arg_r = n[:, None] * ar[:, :, None, :]
        arg_i = n[:, None] * ai[:, :, None, :]
        mag = jnp.exp(arg_r)
        return mag * jnp.cos(arg_i), mag * jnp.sin(arg_i)

    lbr, lbi = power(1.0)
    den = lr * lr + li * li
    qr = ((lbr - 1.0) * lr + lbi * li) / den
    qi = (lbi * lr - (lbr - 1.0) * li) / den
    bre, bim = b_re.astype(F32)[None], b_im.astype(F32)[None]
    bbr = qr[..., None] * bre - qi[..., None] * bim
    bbi = qr[..., None] * bim + qi[..., None] * bre
    cr, ci = c_re.astype(F32), c_im.astype(F32)

    steps = jnp.arange(t, dtype=F32)
    pwr, pwi = power(steps[:, None, None, None])
    bbr_t, bbi_t = jnp.swapaxes(bbr, -1, -2), jnp.swapaxes(bbi, -1, -2)
    prod_re = cr[:, :, :, None, :] * bbr_t[:, :, None, :, :] - ci[:, :, :, None, :] * bbi_t[:, :, None, :, :]
    prod_im = cr[:, :, :, None, :] * bbi_t[:, :, None, :, :] + ci[:, :, :, None, :] * bbr_t[:, :, None, :, :]
    kern = (jnp.einsum('kdgp,dgoip->kdgoi', pwr, prod_re, precision=hp)
            - jnp.einsum('kdgp,dgoip->kdgoi', pwi, prod_im, precision=hp))
    kf, kb = kern[:, 0], kern[:, 1]
    kall = jnp.concatenate([kb[:0:-1], (kf[0] + kb[0])[None], kf[1:], jnp.zeros_like(kf[:1])], axis=0)
    m = _s5_toeplitz(jnp.transpose(kall, (1, 3, 2, 0)))

    def lane_pad(v):
        return jnp.pad(v, [(0, 0)] * (v.ndim - 1) + [(0, p)])

    def state_maps(exponents, d):
        pr, pi = power_per_step(exponents)
        pr, pi = pr[d][:, None], pi[d][:, None]
        br, bi = bbr_t[d][:, :, None, :], bbi_t[d][:, :, None, :]
        rows = lambda v: lane_pad(v).reshape(g, gc * t, 2 * p).astype(BF16)
        return rows(pr * br - pi * bi), rows(pr * bi + pi * br)

    f_slots = state_maps(t - 1.0 - steps, 0) + state_maps(steps, 1)

    def out_maps(exponents, d):
        pr, pi = power_per_step(exponents)
        pr = jnp.swapaxes(pr[d], -1, -2)[:, None]
        pi = jnp.swapaxes(pi[d], -1, -2)[:, None]
        c_r, c_i = cr[d][..., None], ci[d][..., None]
        cols = lambda v: jnp.pad(v, [(0, 0), (0, 0), (0, p), (0, 0)]).astype(BF16)
        return cols(c_r * pr - c_i * pi), cols(-(c_r * pi + c_i * pr))

    e_slots = out_maps(steps + 1.0, 0) + out_maps(t - steps, 1)
    pad = lambda v, axis: jnp.concatenate([v, jnp.zeros_like(v)], axis=axis)

    step_r, step_i = power(jnp.array([1.0, 2.0, 4.0, 8.0], F32)[:, None, None, None] * t)
    rows_a = []
    for d in range(2):
        for s in range(4):
            rows_a += [step_r[s, d], step_i[s, d]]
    r8 = jnp.arange(8, dtype=F32)[:, None, None, None] * t
    car_r, car_i = power(r8)
    rows = jnp.concatenate([jnp.stack(rows_a, axis=0), car_r[:, 0], car_i[:, 0],
                            car_r[::-1, 1], car_i[::-1, 1]], axis=0)
    tab = pad(jnp.transpose(rows, (1, 0, 2)), 2)
    return m, f_slots, e_slots, tab


def _s5_kernel(u_ref, m_ref, f0_ref, f1_ref, f2_ref, f3_ref, e0_ref, e1_ref, e2_ref, e3_ref,
               tab_ref, y_ref, s_ref, xin_ref):
    bsz, gc, n_c, t = u_ref.shape
    lanes = 2 * S5_STATE
    u = jnp.concatenate(
        [jnp.concatenate([u_ref[b, i] for i in range(gc)], axis=1) for b in range(bsz)], axis=0)
    f_all = jnp.concatenate([r[0] for r in (f0_ref, f1_ref, f2_ref, f3_ref)], axis=1)
    e_all = jnp.concatenate(
        [jnp.concatenate([r[0, o] for o in range(gc)], axis=1)
         for r in (e0_ref, e1_ref, e2_ref, e3_ref)], axis=0)
    s_ref[...] = _dot(u, f_all)

    row = lax.broadcasted_iota(jnp.int32, (8, lanes), 0)
    tab_a = tab_ref[0, 0:16, :]
    bc = lambda v: jnp.broadcast_to(v, (8, lanes))
    step_w = [[(bc(tab_a[8 * d + 2 * s:8 * d + 2 * s + 1]), bc(tab_a[8 * d + 2 * s + 1:8 * d + 2 * s + 2]))
               for s in range(4)] for d in range(2)]
    car = [(tab_ref[0, 16 + 16 * d:24 + 16 * d, :], tab_ref[0, 24 + 16 * d:32 + 16 * d, :])
           for d in range(2)]

    def shifted(v, sh, d):
        if d == 0:
            return jnp.where(row >= sh, pltpu.roll(v, sh, 0), 0.0)
        return jnp.where(row < 8 - sh, pltpu.roll(v, 8 - sh, 0), 0.0)

    def tile_scan(r0, hr, hi, d):
        c0 = 2 * lanes * d
        zr = s_ref[r0:r0 + 8, c0:c0 + lanes]
        zi = s_ref[r0:r0 + 8, c0 + lanes:c0 + 2 * lanes]
        for s, sh in enumerate((1, 2, 4)):
            wr, wi = step_w[d][s]
            sr, si = shifted(zr, sh, d), shifted(zi, sh, d)
            zr, zi = zr + wr * sr - wi * si, zi + wr * si + wi * sr
        pr, pi = car[d]
        xin_ref[r0:r0 + 8, c0:c0 + lanes] = pr * hr - pi * hi + shifted(zr, 1, d)
        xin_ref[r0:r0 + 8, c0 + lanes:c0 + 2 * lanes] = pr * hi + pi * hr + shifted(zi, 1, d)
        last = 0 if d else 7
        wr, wi = step_w[d][3]
        return (wr * hr - wi * hi + bc(zr[last:last + 1]), wr * hi + wi * hr + bc(zi[last:last + 1]))

    n_t = n_c // 8
    zero = jnp.zeros((8, lanes), F32)
    for b in range(bsz):
        fr, fi, br, bi = zero, zero, zero, zero
        for k in range(n_t):
            fr, fi = tile_scan(b * n_c + 8 * k, fr, fi, 0)
            br, bi = tile_scan(b * n_c + 8 * (n_t - 1 - k), br, bi, 1)

    y = _dot(u, m_ref[0]) + _dot(xin_ref[...].astype(BF16), e_all)
    for b in range(bsz):
        for o in range(gc):
            y_ref[b, o] = y[b * n_c:(b + 1) * n_c, o * t:(o + 1) * t].astype(BF16)


def _s5_scan(u_t, m, f_slots, e_slots, tab):
    bsz, width, seqlen = u_t.shape
    n_c = seqlen // S5_T
    gc = S5_GROUP_CH
    rows = bsz * n_c
    f_spec = pl.BlockSpec((1, gc * S5_T, 2 * S5_STATE), lambda q: (q, 0, 0))
    e_spec = pl.BlockSpec((1, gc, 2 * S5_STATE, S5_T), lambda q: (q, 0, 0, 0))
    y = pl.pallas_call(
        _s5_kernel,
        grid=(S5_GROUPS,),
        in_specs=[
            pl.BlockSpec((bsz, gc, n_c, S5_T), lambda q: (0, q, 0, 0)),
            pl.BlockSpec((1, gc * S5_T, gc * S5_T), lambda q: (q, 0, 0)),
            f_spec, f_spec, f_spec, f_spec, e_spec, e_spec, e_spec, e_spec,
            pl.BlockSpec((1, 48, 2 * S5_STATE), lambda q: (q, 0, 0)),
        ],
        out_specs=pl.BlockSpec((bsz, gc, n_c, S5_T), lambda q: (0, q, 0, 0)),
        out_shape=jax.ShapeDtypeStruct((bsz, width, n_c, S5_T), BF16),
        scratch_shapes=[pltpu.VMEM((rows, 8 * S5_STATE), F32), pltpu.VMEM((rows, 8 * S5_STATE), F32)],
        compiler_params=_params("parallel"),
    )(u_t.reshape(bsz, width, n_c, S5_T), m, *f_slots, *e_slots, tab)
    return y.reshape(bsz, width, seqlen)


def _gelu_tanh(v):
    return 0.5 * v * (1.0 + jnp.tanh(math.sqrt(2.0 / math.pi) * (v + 0.044715 * (v * v * v))))


def _outproj_kernel(yssd_ref, ys5_ref, u_ref, x_ref, g1_ref, sc2_ref, sh2_ref, d5_ref,
                    wglu_t_ref, bglu_ref, wout_ref, lng_ref, lnb_ref, x1_ref, h2_ref):
    y = ys5_ref[0].astype(F32) + u_ref[0].astype(F32) * d5_ref[...]
    y = _gelu_tanh(y)
    s5 = y * jax.nn.sigmoid(_dot(wglu_t_ref[...], y.astype(BF16)) + bglu_ref[...])
    mix = (_dot(yssd_ref[0], wout_ref[:SSD_WIDTH, :])
           + _dot_tn(s5.astype(BF16), wout_ref[SSD_WIDTH:, :]))
    x1 = _layer_norm(DEEPNORM_ALPHA * x_ref[0] + g1_ref[0] * mix, lng_ref[...], lnb_ref[...])
    x1_ref[0] = x1
    h2_ref[0] = (x1 * (1.0 + sc2_ref[0]) + sh2_ref[0]).astype(BF16)


def _outproj(yssd, ys5_t, u_t, x, g1, sc2, sh2, d5, wglu, bglu, wout, lng, lnb, tm):
    bsz, seqlen, _ = x.shape
    tok = lambda b, i: (b, i, 0)
    tok_t = lambda b, i: (b, 0, i)
    per_b = lambda b, i: (b, 0, 0)
    const = lambda b, i: (0, 0)
    row = lambda v: v.astype(F32).reshape(1, -1)
    col = lambda v: v.astype(F32).reshape(-1, 1)
    return pl.pallas_call(
        _outproj_kernel,
        grid=(bsz, seqlen // tm),
        in_specs=[
            pl.BlockSpec((1, tm, SSD_WIDTH), tok),
            pl.BlockSpec((1, S5_WIDTH, tm), tok_t),
            pl.BlockSpec((1, S5_WIDTH, tm), tok_t),
            pl.BlockSpec((1, tm, D_MODEL), tok),
            pl.BlockSpec((1, 1, D_MODEL), per_b),
            pl.BlockSpec((1, 1, D_MODEL), per_b),
            pl.BlockSpec((1, 1, D_MODEL), per_b),
            pl.BlockSpec((S5_WIDTH, 1), const),
            pl.BlockSpec((S5_WIDTH, S5_WIDTH), const),
            pl.BlockSpec((S5_WIDTH, 1), const),
            pl.BlockSpec((D_MODEL, D_MODEL), const),
            pl.BlockSpec((1, D_MODEL), const),
            pl.BlockSpec((1, D_MODEL), const),
        ],
        out_specs=[pl.BlockSpec((1, tm, D_MODEL), tok), pl.BlockSpec((1, tm, D_MODEL), tok)],
        out_shape=[jax.ShapeDtypeStruct((bsz, seqlen, D_MODEL), F32),
                   jax.ShapeDtypeStruct((bsz, seqlen, D_MODEL), BF16)],
        compiler_params=_params("parallel", "parallel"),
    )(yssd, ys5_t, u_t, x, g1, sc2, sh2, col(d5), wglu.T.astype(BF16), col(bglu), wout.astype(BF16),
      row(lng), row(lnb))


def _round_up(v, m):
    return (v + m - 1) // m * m


def _route_subtile(h2_s, wr_t, br, tri):
    s_len = h2_s.shape[0]
    logits = _dot_nt(wr_t, h2_s) + br
    mx = jnp.max(logits, axis=0, keepdims=True)
    ex = jnp.exp(logits - mx)
    probs = ex / jnp.sum(ex, axis=0, keepdims=True)
    row = lax.broadcasted_iota(jnp.int32, (N_EXPERTS, s_len), 0)
    n_grp = N_EXPERTS // EXPERTS_PER_GROUP
    gs = [jnp.max(probs[EXPERTS_PER_GROUP * g:EXPERTS_PER_GROUP * (g + 1)], axis=0, keepdims=True)
          for g in range(n_grp)]
    gmax = functools.reduce(jnp.maximum, gs)
    sel_grp = jnp.full((1, s_len), n_grp - 1, jnp.int32)
    for g in range(n_grp - 2, -1, -1):
        sel_grp = jnp.where(gs[g] == gmax, g, sel_grp)
    in_grp = (row // EXPERTS_PER_GROUP) == sel_grp
    masked = jnp.where(in_grp, probs, -1.0)
    m1 = jnp.max(masked, axis=0, keepdims=True)
    i1 = jnp.min(jnp.where(masked == m1, row, N_EXPERTS), axis=0, keepdims=True)
    is1 = row == i1
    masked2 = jnp.where(is1, -2.0, masked)
    m2 = jnp.max(masked2, axis=0, keepdims=True)
    i2 = jnp.min(jnp.where(masked2 == m2, row, N_EXPERTS), axis=0, keepdims=True)
    is2 = row == i2
    den = m1 + m2
    sel = jnp.where(is1, 1.0, jnp.where(is2, 1.0, 0.0))
    pos = _dot(sel.astype(BF16), tri)
    return is1, is2, m1 / den, m2 / den, sel, pos


def _copy_rows(src_ref, src0, dst_ref, dst0, n_chunks):
    def body(c, carry):
        so = pl.multiple_of(src0 + c * MOE_ALIGN, MOE_ALIGN)
        do = pl.multiple_of(dst0 + c * MOE_ALIGN, MOE_ALIGN)
        dst_ref[pl.ds(do, MOE_ALIGN), :] = src_ref[pl.ds(so, MOE_ALIGN), :]
        return carry
    lax.fori_loop(0, n_chunks, body, 0)


def _zero_rows(dst_ref, dst0, n_chunks):
    def body(c, carry):
        do = pl.multiple_of(dst0 + c * MOE_ALIGN, MOE_ALIGN)
        dst_ref[pl.ds(do, MOE_ALIGN), :] = jnp.zeros((MOE_ALIGN, dst_ref.shape[1]), dst_ref.dtype)
        return carry
    lax.fori_loop(0, n_chunks, body, 0)


def _moe_kernel(h2_ref, x1_ref, g2_ref, wr_t_ref, br_ref, tri_ref, wg_ref, wu_ref, wd_ref,
                lng_ref, lnb_ref, out_ref, xg_ref, cb_ref, route_ref, meta_ref):
    step = pl.program_id(2)
    tm = h2_ref.shape[1]
    s_len = x1_ref.shape[1]
    n_sub = tm // s_len
    cb_rows = cb_ref.shape[0]
    ne = N_EXPERTS
    seg_len, seg_cb, seg_xg, e_base, e_blk, sub_rows = 0, 64, 128, 192, 208, 224

    def onehot_rows(r0, n_rows, t1, t2, v1, v2):
        slot = (lax.broadcasted_iota(jnp.int32, (n_rows, s_len), 0) + r0).astype(F32)
        return jnp.where(slot == t1, v1, jnp.where(slot == t2, v2, 0.0))

    @pl.when(step == 0)
    def _dispatch():
        erow = lax.broadcasted_iota(jnp.int32, (ne, 1), 0)
        for s in range(n_sub):
            h2_s = h2_ref[0, s * s_len:(s + 1) * s_len, :]
            is1, is2, g1, g2, sel, pos = _route_subtile(h2_s, wr_t_ref[...], br_ref[...], tri_ref[...])
            base_vec = jnp.zeros((ne, 1), F32)
            run = jnp.int32(0)
            for e in range(ne):
                n16 = _round_up(jnp.sum(sel[e:e + 1, :]).astype(jnp.int32), MOE_ALIGN)
                meta_ref[seg_len + s * ne + e] = n16
                meta_ref[seg_cb + s * ne + e] = run
                base_vec = jnp.where(erow == e, run.astype(F32), base_vec)
                run = run + n16
            meta_ref[sub_rows + s] = run
            tgt = base_vec + pos
            route_ref[s, 0:1, :] = jnp.sum(jnp.where(is1, tgt, 0.0), axis=0, keepdims=True)
            route_ref[s, 1:2, :] = jnp.sum(jnp.where(is2, tgt, 0.0), axis=0, keepdims=True)
            route_ref[s, 2:3, :] = g1
            route_ref[s, 3:4, :] = g2
        run = jnp.int32(0)
        for e in range(ne):
            rows_e = jnp.int32(0)
            for s in range(n_sub):
                meta_ref[seg_xg + s * ne + e] = run + rows_e
                rows_e = rows_e + meta_ref[seg_len + s * ne + e]
            blocks = (rows_e + MOE_ROWS - 1) // MOE_ROWS
            meta_ref[e_base + e] = run
            meta_ref[e_blk + e] = blocks
            _zero_rows(xg_ref, run + rows_e, (blocks * MOE_ROWS - rows_e) // MOE_ALIGN)
            run = run + blocks * MOE_ROWS
        for s in range(n_sub):
            h2_s = h2_ref[0, s * s_len:(s + 1) * s_len, :]
            t1, t2 = route_ref[s, 0:1, :], route_ref[s, 1:2, :]
            for r0 in range(0, cb_rows, MOE_GATHER_ROWS):
                onehot = onehot_rows(r0, MOE_GATHER_ROWS, t1, t2, 1.0, 1.0).astype(BF16)
                cb_ref[r0:r0 + MOE_GATHER_ROWS, :] = _dot(onehot, h2_s).astype(BF16)
            for e in range(ne):
                _copy_rows(cb_ref, meta_ref[seg_cb + s * ne + e], xg_ref, meta_ref[seg_xg + s * ne + e],
                           meta_ref[seg_len + s * ne + e] // MOE_ALIGN)

    n_expert_steps = ne // MOE_EXPERTS_PER_STEP

    def run_expert(k):
        e = step * MOE_EXPERTS_PER_STEP + k
        base = meta_ref[e_base + e]
        n_blk = meta_ref[e_blk + e]

        def run_rows(r0, rows):
            xe = xg_ref[pl.ds(r0, rows), :]
            hid = (_silu(_dot(xe, wg_ref[k])) * _dot(xe, wu_ref[k])).astype(BF16)
            xg_ref[pl.ds(r0, rows), :] = _dot(hid, wd_ref[k]).astype(BF16)

        def pair(j, carry):
            run_rows(pl.multiple_of(base + j * 2 * MOE_ROWS, MOE_ROWS), 2 * MOE_ROWS)
            return carry

        lax.fori_loop(0, n_blk // 2, pair, 0)

        @pl.when(n_blk % 2 == 1)
        def _tail():
            run_rows(pl.multiple_of(base + (n_blk - 1) * MOE_ROWS, MOE_ROWS), MOE_ROWS)

    @pl.when(step < n_expert_steps)
    def _experts():
        for k in range(MOE_EXPERTS_PER_STEP):
            run_expert(k)

    @pl.when(step >= n_expert_steps)
    def _combine():
        s = step - n_expert_steps
        for e in range(ne):
            _copy_rows(xg_ref, meta_ref[seg_xg + s * ne + e], cb_ref, meta_ref[seg_cb + s * ne + e],
                       meta_ref[seg_len + s * ne + e] // MOE_ALIGN)
        used = meta_ref[sub_rows + s]
        _zero_rows(cb_ref, used, (cb_rows - used) // MOE_ALIGN)
        t1, t2 = route_ref[s, 0:1, :], route_ref[s, 1:2, :]
        g1, g2 = route_ref[s, 2:3, :], route_ref[s, 3:4, :]
        acc = jnp.zeros((s_len, D_MODEL), F32)
        for r0 in range(0, cb_rows, MOE_GATHER_ROWS):
            weighted = onehot_rows(r0, MOE_GATHER_ROWS, t1, t2, g1, g2).astype(BF16)
            acc = acc + _dot_tn(weighted, cb_ref[r0:r0 + MOE_GATHER_ROWS, :])
        v = DEEPNORM_ALPHA * x1_ref[0] + g2_ref[0] * acc
        out_ref[0] = _layer_norm(v, lng_ref[...], lnb_ref[...])


def _moe(h2, x1, g2, wr_t, br, tri, wg, wu, wd, lng, lnb, tm, s_len):
    bsz, seqlen, _ = x1.shape
    n_sub = tm // s_len
    ne = N_EXPERTS
    cb_rows = _round_up(2 * s_len + ne * (MOE_ALIGN - 1), MOE_GATHER_ROWS)
    xg_rows = _round_up(2 * tm + n_sub * ne * (MOE_ALIGN - 1) + ne * (MOE_ROWS - 1), MOE_ROWS)
    tok = lambda b, i, st: (b, i, 0)
    n_es = ne // MOE_EXPERTS_PER_STEP
    sub = lambda b, i, st: (b, i * n_sub + jnp.clip(st - n_es, 0, n_sub - 1), 0)
    per_b = lambda b, i, st: (b, 0, 0)
    const = lambda b, i, st: (0, 0)
    per_e = lambda b, i, st: (jnp.minimum(st, n_es - 1), 0, 0)
    row = lambda v: v.astype(F32).reshape(1, -1)
    return pl.pallas_call(
        _moe_kernel,
        grid=(bsz, seqlen // tm, n_es + n_sub),
        in_specs=[
            pl.BlockSpec((1, tm, D_MODEL), tok),
            pl.BlockSpec((1, s_len, D_MODEL), sub),
            pl.BlockSpec((1, 1, D_MODEL), per_b),
            pl.BlockSpec((ne, D_MODEL), const),
            pl.BlockSpec((ne, 1), const),
            pl.BlockSpec((s_len, s_len), const),
            pl.BlockSpec((MOE_EXPERTS_PER_STEP, D_MODEL, D_EXPERT), per_e),
            pl.BlockSpec((MOE_EXPERTS_PER_STEP, D_MODEL, D_EXPERT), per_e),
            pl.BlockSpec((MOE_EXPERTS_PER_STEP, D_EXPERT, D_MODEL), per_e),
            pl.BlockSpec((1, D_MODEL), const),
            pl.BlockSpec((1, D_MODEL), const),
        ],
        out_specs=pl.BlockSpec((1, s_len, D_MODEL), sub),
        out_shape=jax.ShapeDtypeStruct((bsz, seqlen, D_MODEL), F32),
        scratch_shapes=[pltpu.VMEM((xg_rows, D_MODEL), BF16), pltpu.VMEM((cb_rows, D_MODEL), BF16),
                        pltpu.VMEM((n_sub, 8, s_len), F32), pltpu.SMEM((256,), jnp.int32)],
        compiler_params=_params("parallel", "parallel", "arbitrary"),
    )(h2, x1, g2, wr_t, br, tri, wg, wu, wd, row(lng), row(lnb))


def _layer_weights(i, w):
    w_in = w['w_in'][i]
    n_dt = 2 * SSD_HEADS
    dt_lo = SSD_WIDTH + SSD_CONV_CH
    w_dt = w_in[:, dt_lo:dt_lo + n_dt]
    w_u = w_in[:, dt_lo + n_dt:]
    w_all = jnp.concatenate(
        [w_in[:, :dt_lo], w_dt, jnp.zeros((D_MODEL, DT_PAD - n_dt), w_in.dtype)], axis=1).astype(BF16)
    w_t = jnp.concatenate([w_u, w_dt], axis=1).T.astype(BF16)
    m, f, e, tab = _s5_tables(w['s5_lam_re'][i], w['s5_lam_im'][i], w['s5_log_step'][i],
                              w['s5_b_re'][i], w['s5_b_im'][i], w['s5_c_re'][i], w['s5_c_im'][i])
    return dict(
        w_all=w_all, w_t=w_t,
        s5=(m, f, e, tab),
        wg=w['w_gate'][i].astype(BF16), wu=w['w_up'][i].astype(BF16), wd=w['w_down'][i].astype(BF16),
    )


def _trunk(x, mods, w, layer_w, tiles):
    bsz, seqlen, _ = x.shape
    tm, tq, tmoe = tiles
    wr_t = w['w_router'].T.astype(BF16)
    br = w['b_router'].astype(F32).reshape(N_EXPERTS, 1)
    s_len = min(MOE_SUB, tmoe)
    ids = jnp.arange(s_len)
    tri = (ids[:, None] < ids[None, :]).astype(BF16)
    for i in range(DEPTH):
        lw = layer_w[i]
        sh1, sc1, g1, sh2, sc2, g2 = [v.reshape(bsz, 1, D_MODEL) for v in jnp.split(mods[i], 6, axis=-1)]
        z, xbc, dt, u_t, dt_t = _inproj(x, sc1, sh1, lw['w_all'], lw['w_t'], tm)
        xbc = _conv_silu(xbc, w['conv_w'][i], w['conv_b'][i], tm)
        y_f = _ssd_dir(xbc, dt, dt_t, w['ssd_a_log'][i], w['ssd_dt_bias'][i], False, tq)
        y_ssd = _ssd_dir(xbc, dt, dt_t, w['ssd_a_log'][i], w['ssd_dt_bias'][i], True, tq,
                         z=z, yf=y_f, d_skip=w['ssd_d'][i], norm_w=w['ssd_norm_w'][i])
        y_s5_t = _s5_scan(u_t, *lw['s5'])
        x1, h2 = _outproj(y_ssd, y_s5_t, u_t, x, g1, sc2, sh2, w['s5_d'][i], w['s5_w_glu'][i],
                          w['s5_b_glu'][i], w['w_out'][i], w['ln1_g'][i], w['ln1_b'][i], tm)
        x = _moe(h2, x1, g2, wr_t, br, tri, lw['wg'], lw['wu'], lw['wd'],
                 w['ln2_g'][i], w['ln2_b'][i], tmoe, s_len)
    return x


def _pick(n, pref):
    return pref if n % pref == 0 else n


def _run(x_groups, c_groups, w):
    rows = sum(c.shape[0] for c in c_groups)
    pad = (-rows) % 8
    c_all = jnp.concatenate(list(c_groups) + [jnp.zeros((pad, D_MODEL), F32)], axis=0).astype(F32)
    mods = _ada_mod(c_all, w['w_ada'].astype(F32), w['b_ada'].astype(F32))
    layer_w = [_layer_weights(i, w) for i in range(DEPTH)]
    outs = []
    r0 = 0
    for x, c in zip(x_groups, c_groups):
        bsz, seqlen, _ = x.shape
        assert seqlen % (8 * S5_T) == 0, "sequence length must be a multiple of 8 S5 chunks"
        tiles = (_pick(seqlen, 1024), _pick(seqlen, 1024), _pick(seqlen, 2048))
        outs.append(_trunk(x, mods[:, r0:r0 + bsz], w, layer_w, tiles))
        r0 += bsz
    return tuple(outs)


def kernel(x_prompt, x_sample, c_prompt, c_sample, w_ada, b_ada, w_in, conv_w, conv_b, ssd_a_log, ssd_dt_bias, ssd_d, ssd_norm_w, s5_lam_re, s5_lam_im, s5_log_step, s5_b_re, s5_b_im, s5_c_re, s5_c_im, s5_d, s5_w_glu, s5_b_glu, w_out, ln1_g, ln1_b, ln2_g, ln2_b, w_router, b_router, w_gate, w_up, w_down):
    w = dict(w_ada=w_ada, b_ada=b_ada, w_in=w_in, conv_w=conv_w, conv_b=conv_b, ssd_a_log=ssd_a_log,
             ssd_dt_bias=ssd_dt_bias, ssd_d=ssd_d, ssd_norm_w=ssd_norm_w, s5_lam_re=s5_lam_re,
             s5_lam_im=s5_lam_im, s5_log_step=s5_log_step, s5_b_re=s5_b_re, s5_b_im=s5_b_im,
             s5_c_re=s5_c_re, s5_c_im=s5_c_im, s5_d=s5_d, s5_w_glu=s5_w_glu, s5_b_glu=s5_b_glu,
             w_out=w_out, ln1_g=ln1_g, ln1_b=ln1_b, ln2_g=ln2_g, ln2_b=ln2_b, w_router=w_router,
             b_router=b_router, w_gate=w_gate, w_up=w_up, w_down=w_down)
    return _run((x_prompt, x_sample), (c_prompt, c_sample), w)
```

```python
import functools
import math

import jax
import jax.numpy as jnp
from jax import lax
from jax.experimental import pallas as pl
from jax.experimental.pallas import tpu as pltpu

F32 = jnp.float32
BF16 = jnp.bfloat16

D_MODEL = 1024
DEPTH = 4
SSD_WIDTH = 512
SSD_HEADDIM = 64
SSD_HEADS = 8
SSD_GROUPS = 2
SSD_STATE = 128
SSD_CONV = 5
SSD_CONV_CH = SSD_WIDTH + 2 * SSD_GROUPS * SSD_STATE
S5_WIDTH = 512
S5_GROUP_CH = 16
S5_GROUPS = 32
S5_STATE = 64
S5_MIN_DECAY = 1e-4
N_EXPERTS = 16
EXPERTS_PER_GROUP = 4
D_EXPERT = 512
DEEPNORM_ALPHA = (2 * DEPTH) ** 0.25
LN_EPS = 1e-5
RMS_EPS = 1e-5

SSD_Q = 128
S5_T = 128
PROJ_COLS = SSD_WIDTH + SSD_CONV_CH
DT_PAD = 128
CONV_HALO = 16
MOE_ROWS = 128
MOE_ALIGN = 16
MOE_GATHER_ROWS = 256
MOE_SUB = 512
MOE_EXPERTS_PER_STEP = 2
VMEM_LIMIT = 56 * 1024 * 1024


def _dot(a, b):
    return jnp.dot(a, b, preferred_element_type=F32)


def _dot_nt(a, b):
    return lax.dot_general(a, b, (((1,), (1,)), ((), ())), preferred_element_type=F32)


def _dot_tn(a, b):
    return lax.dot_general(a, b, (((0,), (0,)), ((), ())), preferred_element_type=F32)


def _split_bf16(v):
    hi = v.astype(BF16)
    lo = (v - hi.astype(F32)).astype(BF16)
    return hi, lo


def _silu(v):
    return v * jax.nn.sigmoid(v)


def _layer_norm(v, g, b):
    mu = jnp.mean(v, axis=-1, keepdims=True)
    vc = v - mu
    var = jnp.mean(vc * vc, axis=-1, keepdims=True)
    return vc * lax.rsqrt(var + LN_EPS) * g + b


def _params(*sem):
    return pltpu.CompilerParams(dimension_semantics=sem, vmem_limit_bytes=VMEM_LIMIT)


def _ada_kernel(c_ref, w_ref, b_ref, o_ref):
    ca = _silu(c_ref[...])
    c_hi, c_lo = _split_bf16(ca)
    w_hi, w_lo = _split_bf16(w_ref[0])
    acc = _dot(c_hi, w_hi) + _dot(c_lo, w_hi) + _dot(c_hi, w_lo)
    o_ref[0] = acc + b_ref[0]


def _ada_mod(c_all, w_ada, b_ada):
    rows = c_all.shape[0]
    n_out = w_ada.shape[-1]
    bn = 1536
    return pl.pallas_call(
        _ada_kernel,
        grid=(DEPTH, n_out // bn),
        in_specs=[
            pl.BlockSpec((rows, D_MODEL), lambda l, j: (0, 0)),
            pl.BlockSpec((1, D_MODEL, bn), lambda l, j: (l, 0, j)),
            pl.BlockSpec((1, 1, bn), lambda l, j: (l, 0, j)),
        ],
        out_specs=pl.BlockSpec((1, rows, bn), lambda l, j: (l, 0, j)),
        out_shape=jax.ShapeDtypeStruct((DEPTH, rows, n_out), F32),
        compiler_params=_params("arbitrary", "arbitrary"),
    )(c_all, w_ada, b_ada.reshape(DEPTH, 1, n_out))


def _inproj_kernel(x_ref, xl_ref, xr_ref, sc_ref, sh_ref, w_ref, w_t_ref, cw_ref, cb_ref,
                   z_ref, xbc_ref, dt_ref, u_t_ref, dt_t_ref, ext_ref):
    i = pl.program_id(1)
    n = pl.num_programs(1)
    tm = x_ref.shape[1]
    scale = 1.0 + sc_ref[0]
    shift = sh_ref[0]
    h = (x_ref[0] * scale + shift).astype(BF16)
    p = _dot(h, w_ref[...])
    z_ref[0] = p[:, :SSD_WIDTH].astype(BF16)
    dt_ref[0] = p[:, PROJ_COLS:PROJ_COLS + 2 * SSD_HEADS]
    p_t = _dot_nt(w_t_ref[...], h)
    u_t_ref[0] = p_t[:S5_WIDTH].astype(BF16)
    dt_t_ref[0] = p_t[S5_WIDTH:]

    halo = jnp.concatenate([xl_ref[0], xr_ref[0]], axis=0)
    p_halo = _dot((halo * scale + shift).astype(BF16), w_ref[:, SSD_WIDTH:PROJ_COLS])
    has_left = (i > 0).astype(F32)
    has_right = (i < n - 1).astype(F32)
    ext_ref[0:CONV_HALO, :] = p_halo[:CONV_HALO] * has_left
    ext_ref[CONV_HALO:CONV_HALO + tm, :] = p[:, SSD_WIDTH:PROJ_COLS]
    ext_ref[CONV_HALO + tm:2 * CONV_HALO + tm, :] = p_halo[CONV_HALO:] * has_right
    acc = jnp.zeros((tm, SSD_CONV_CH), F32) + cb_ref[...]
    for k in range(SSD_CONV):
        start = CONV_HALO - SSD_CONV // 2 + k
        acc = acc + ext_ref[start:start + tm, :] * cw_ref[k:k + 1, :]
    xbc_ref[0] = _silu(acc).astype(BF16)


def _inproj(x, sc, sh, w_all, w_t, conv_w, conv_b, tm):
    bsz, seqlen, _ = x.shape
    per_blk = tm // CONV_HALO
    n_halo = seqlen // CONV_HALO
    tok = lambda b, i: (b, i, 0)
    tok_t = lambda b, i: (b, 0, i)
    per_b = lambda b, i: (b, 0, 0)
    const = lambda b, i: (0, 0)
    w_pad = jnp.zeros((8, SSD_CONV_CH), F32).at[:SSD_CONV].set(conv_w.astype(F32))
    return pl.pallas_call(
        _inproj_kernel,
        grid=(bsz, seqlen // tm),
        in_specs=[
            pl.BlockSpec((1, tm, D_MODEL), tok),
            pl.BlockSpec((1, CONV_HALO, D_MODEL), lambda b, i: (b, jnp.maximum(i * per_blk - 1, 0), 0)),
            pl.BlockSpec((1, CONV_HALO, D_MODEL),
                         lambda b, i: (b, jnp.minimum((i + 1) * per_blk, n_halo - 1), 0)),
            pl.BlockSpec((1, 1, D_MODEL), per_b),
            pl.BlockSpec((1, 1, D_MODEL), per_b),
            pl.BlockSpec(w_all.shape, const),
            pl.BlockSpec(w_t.shape, const),
            pl.BlockSpec((8, SSD_CONV_CH), const),
            pl.BlockSpec((1, SSD_CONV_CH), const),
        ],
        out_specs=[
            pl.BlockSpec((1, tm, SSD_WIDTH), tok),
            pl.BlockSpec((1, tm, SSD_CONV_CH), tok),
            pl.BlockSpec((1, tm, 2 * SSD_HEADS), tok),
            pl.BlockSpec((1, S5_WIDTH, tm), tok_t),
            pl.BlockSpec((1, 2 * SSD_HEADS, tm), tok_t),
        ],
        out_shape=[
            jax.ShapeDtypeStruct((bsz, seqlen, SSD_WIDTH), BF16),
            jax.ShapeDtypeStruct((bsz, seqlen, SSD_CONV_CH), BF16),
            jax.ShapeDtypeStruct((bsz, seqlen, 2 * SSD_HEADS), F32),
            jax.ShapeDtypeStruct((bsz, S5_WIDTH, seqlen), BF16),
            jax.ShapeDtypeStruct((bsz, 2 * SSD_HEADS, seqlen), F32),
        ],
        scratch_shapes=[pltpu.VMEM((tm + 2 * CONV_HALO, SSD_CONV_CH), F32)],
        compiler_params=_params("parallel", "parallel"),
    )(x, x, x, sc, sh, w_all, w_t, w_pad, conv_b.astype(F32).reshape(1, SSD_CONV_CH))


def _softplus(v):
    return jnp.maximum(v, 0.0) + jnp.log1p(jnp.exp(-jnp.abs(v)))


def _head_indicator(width):
    head = lax.broadcasted_iota(jnp.int32, (SSD_HEADS, SSD_HEADS * width), 0)
    lane = lax.broadcasted_iota(jnp.int32, (SSD_HEADS, SSD_HEADS * width), 1)
    return jnp.where(lane // width == head, 1.0, 0.0).astype(BF16)


def _ssd_chunk(xbc, dt_raw, dt_raw_t, a_row, a_col, bias_row, bias_col, state_ref, rev):
    q = xbc.shape[0]
    x = xbc[:, :SSD_WIDTH]
    b_mat = xbc[:, SSD_WIDTH:SSD_WIDTH + SSD_GROUPS * SSD_STATE]
    c_mat = xbc[:, SSD_WIDTH + SSD_GROUPS * SSD_STATE:]
    dts = _softplus(dt_raw + bias_row)
    dts_t = _softplus(dt_raw_t + bias_col)
    dta = dts * a_row
    dta_t = dts_t * a_col

    row = lax.broadcasted_iota(jnp.int32, (q, q), 0)
    col = lax.broadcasted_iota(jnp.int32, (q, q), 1)
    lower = col <= row
    upper = col >= row
    tri_lower = jnp.where(lower, 1.0, 0.0).astype(BF16)
    tri_upper = jnp.where(upper, 1.0, 0.0).astype(BF16)
    tri_c, tri_r, mask = (tri_upper, tri_lower, upper) if rev else (tri_lower, tri_upper, lower)

    d_hi, d_lo = _split_bf16(dta)
    acc = _dot(tri_c, d_hi) + _dot(tri_c, d_lo)
    t_hi, t_lo = _split_bf16(dta_t)
    acc_t = _dot(t_hi, tri_r) + _dot(t_lo, tri_r)

    lane128 = lax.broadcasted_iota(jnp.int32, (q, 128), 1)
    x32 = x.astype(F32)
    cb = [_dot_nt(c_mat[:, g * SSD_STATE:(g + 1) * SSD_STATE],
                  b_mat[:, g * SSD_STATE:(g + 1) * SSD_STATE]) for g in range(SSD_GROUPS)]
    a_hi, a_lo = _split_bf16(acc)
    ind_q = _head_indicator(q)
    t_row = jnp.concatenate([acc_t[h:h + 1, :] for h in range(SSD_HEADS)], axis=1)
    seg_all = _dot(a_hi, ind_q) + _dot(a_lo, ind_q) - t_row
    ys = []
    for k in range(SSD_HEADS // 2):
        m_pair = []
        for hh in range(2):
            h = 2 * k + hh
            g = h // (SSD_HEADS // SSD_GROUPS)
            seg = seg_all[:, h * q:(h + 1) * q]
            dec = jnp.where(mask, jnp.exp(jnp.minimum(seg, 0.0)), 0.0)
            m_pair.append((cb[g] * dec * dts_t[h:h + 1, :]).astype(BF16))
        xp = x32[:, 128 * k:128 * (k + 1)]
        x_blk = jnp.concatenate([jnp.where(lane128 < SSD_HEADDIM, xp, 0.0).astype(BF16),
                                 jnp.where(lane128 < SSD_HEADDIM, 0.0, xp).astype(BF16)], axis=0)
        ys.append(_dot(jnp.concatenate(m_pair, axis=1), x_blk))
    y = jnp.concatenate(ys, axis=1)

    end = acc[0:1, :] if rev else acc[q - 1:q, :]
    ind_p = _head_indicator(SSD_HEADDIM)
    y_scale = _dot(jnp.exp(acc).astype(BF16), ind_p)
    x_w = _dot((dts * jnp.exp(end - acc)).astype(BF16), ind_p)
    c_hi, c_lo = _split_bf16(jnp.broadcast_to(jnp.exp(end), (8, SSD_HEADS)))
    carry = (_dot(c_hi, ind_p) + _dot(c_lo, ind_p))[0:1, :]
    xw = (x32 * x_w).astype(BF16)
    half = SSD_WIDTH // SSD_GROUPS
    y_off = []
    for g in range(SSD_GROUPS):
        st = state_ref[g]
        y_off.append(_dot(c_mat[:, g * SSD_STATE:(g + 1) * SSD_STATE], st.astype(BF16)))
        state_ref[g] = st * carry[:, g * half:(g + 1) * half] + _dot_tn(
            b_mat[:, g * SSD_STATE:(g + 1) * SSD_STATE], xw[:, g * half:(g + 1) * half])
    return y + jnp.concatenate(y_off, axis=1) * y_scale


def _ssd_dir_kernel(*refs, rev):
    if rev:
        (xbc_ref, dt_ref, dt_t_ref, alog_ref, alog_t_ref, bias_ref, bias_t_ref,
         z_ref, yf_ref, dskip_ref, nw_ref, o_ref, state_ref) = refs
    else:
        (xbc_ref, dt_ref, dt_t_ref, alog_ref, alog_t_ref, bias_ref, bias_t_ref,
         o_ref, state_ref) = refs

    @pl.when(pl.program_id(1) == 0)
    def _():
        state_ref[...] = jnp.zeros(state_ref.shape, F32)

    d = 1 if rev else 0
    hs = slice(d * SSD_HEADS, (d + 1) * SSD_HEADS)
    a_row = -jnp.exp(alog_ref[:, hs])
    a_col = -jnp.exp(alog_t_ref[hs, :])
    bias_row = bias_ref[:, hs]
    bias_col = bias_t_ref[hs, :]
    tq = xbc_ref.shape[1]
    n_chunks = tq // SSD_Q
    order = range(n_chunks - 1, -1, -1) if rev else range(n_chunks)
    for ci in order:
        rs = slice(ci * SSD_Q, (ci + 1) * SSD_Q)
        xbc = xbc_ref[0, rs, :]
        y = _ssd_chunk(xbc, dt_ref[0, rs, hs], dt_t_ref[0, hs, rs], a_row, a_col,
                       bias_row, bias_col, state_ref, rev)
        if rev:
            y = y + yf_ref[0, rs, :].astype(F32) + xbc[:, :SSD_WIDTH].astype(F32) * dskip_ref[...]
            y = y * _silu(z_ref[0, rs, :].astype(F32))
            y = y * lax.rsqrt(jnp.mean(y * y, axis=-1, keepdims=True) + RMS_EPS) * nw_ref[...]
            o_ref[0, rs, :] = y.astype(BF16)
        else:
            o_ref[0, rs, :] = y.astype(BF16)


def _ssd_dir(xbc, dt, dt_t, a_log, dt_bias, rev, tq, z=None, yf=None, d_skip=None, norm_w=None):
    bsz, seqlen, _ = xbc.shape
    nblk = seqlen // tq
    if rev:
        tok = lambda b, i: (b, nblk - 1 - i, 0)
        tok_t = lambda b, i: (b, 0, nblk - 1 - i)
    else:
        tok = lambda b, i: (b, i, 0)
        tok_t = lambda b, i: (b, 0, i)
    const = lambda b, i: (0, 0)
    nh2 = 2 * SSD_HEADS
    a_flat = a_log.astype(F32).reshape(1, nh2)
    bias_flat = dt_bias.astype(F32).reshape(1, nh2)
    args = [xbc, dt, dt_t, a_flat, a_flat.reshape(nh2, 1), bias_flat, bias_flat.reshape(nh2, 1)]
    in_specs = [
        pl.BlockSpec((1, tq, SSD_CONV_CH), tok),
        pl.BlockSpec((1, tq, nh2), tok),
        pl.BlockSpec((1, nh2, tq), tok_t),
        pl.BlockSpec((1, nh2), const),
        pl.BlockSpec((nh2, 1), const),
        pl.BlockSpec((1, nh2), const),
        pl.BlockSpec((nh2, 1), const),
    ]
    if rev:
        args += [z, yf, jnp.repeat(d_skip.astype(F32), SSD_HEADDIM).reshape(1, SSD_WIDTH),
                 norm_w.astype(F32).reshape(1, SSD_WIDTH)]
        in_specs += [
            pl.BlockSpec((1, tq, SSD_WIDTH), tok),
            pl.BlockSpec((1, tq, SSD_WIDTH), tok),
            pl.BlockSpec((1, SSD_WIDTH), const),
            pl.BlockSpec((1, SSD_WIDTH), const),
        ]
    return pl.pallas_call(
        functools.partial(_ssd_dir_kernel, rev=rev),
        grid=(bsz, nblk),
        in_specs=in_specs,
        out_specs=pl.BlockSpec((1, tq, SSD_WIDTH), tok),
        out_shape=jax.ShapeDtypeStruct((bsz, seqlen, SSD_WIDTH), BF16),
        scratch_shapes=[pltpu.VMEM((SSD_GROUPS, SSD_STATE, SSD_WIDTH // SSD_GROUPS), F32)],
        compiler_params=_params("parallel", "arbitrary"),
    )(*args)


def _s5_toeplitz_kernel(k_ref, m_ref):
    t = S5_T
    gc = S5_GROUP_CH
    def rows_of_channel(i, carry):
        r0 = pl.multiple_of(i * t, t)
        for o in range(gc):
            taps = jnp.broadcast_to(k_ref[0, i, o:o + 1, :], (t, 2 * t))
            skew = pltpu.roll(taps, t + 1, 1, stride=1, stride_axis=0)
            m_ref[0, pl.ds(r0, t), o * t:(o + 1) * t] = skew[:, :t].astype(BF16)
        return carry

    lax.fori_loop(0, gc, rows_of_channel, 0)


def _s5_toeplitz(kall):
    g, gc, _, taps = kall.shape
    n = gc * S5_T
    return pl.pallas_call(
        _s5_toeplitz_kernel,
        grid=(g,),
        in_specs=[pl.BlockSpec((1, gc, gc, taps), lambda q: (q, 0, 0, 0))],
        out_specs=pl.BlockSpec((1, n, n), lambda q: (q, 0, 0)),
        out_shape=jax.ShapeDtypeStruct((g, n, n), BF16),
        compiler_params=_params("parallel"),
    )(kall)


def _s5_tables(lam_re, lam_im, log_step, b_re, b_im, c_re, c_im):
    hp = lax.Precision.HIGHEST
    t = S5_T
    g, p, gc = S5_GROUPS, S5_STATE, S5_GROUP_CH
    lr = jnp.minimum(lam_re.astype(F32), -S5_MIN_DECAY)
    li = lam_im.astype(F32)
    step = jnp.exp(log_step.astype(F32))[..., None]
    ar, ai = lr * step, li * step

    def power(n):
        mag = jnp.exp(n * ar)
        return mag * jnp.cos(n * ai), mag * jnp.sin(n * ai)

    def power_per_step(n):
        arg_r = n[:, None] * ar[:, :, None, :]
        arg_i = n[:, None] * ai[:, :, None, :]
        mag = jnp.exp(arg_r)
        return mag * jnp.cos(arg_i), mag * jnp.sin(arg_i)

    lbr, lbi = power(1.0)
    den = lr * lr + li * li
    qr = ((lbr - 1.0) * lr + lbi * li) / den
    qi = (lbi * lr - (lbr - 1.0) * li) / den
    bre, bim = b_re.astype(F32)[None], b_im.astype(F32)[None]
    bbr = qr[..., None] * bre - qi[..., None] * bim
    bbi = qr[..., None] * bim + qi[..., None] * bre
    cr, ci = c_re.astype(F32), c_im.astype(F32)

    steps = jnp.arange(t, dtype=F32)
    pwr, pwi = power(steps[:, None, None, None])
    bbr_t, bbi_t = jnp.swapaxes(bbr, -1, -2), jnp.swapaxes(bbi, -1, -2)
    prod_re = cr[:, :, :, None, :] * bbr_t[:, :, None, :, :] - ci[:, :, :, None, :] * bbi_t[:, :, None, :, :]
    prod_im = cr[:, :, :, None, :] * bbi_t[:, :, None, :, :] + ci[:, :, :, None, :] * bbr_t[:, :, None, :, :]
    kern = (jnp.einsum('kdgp,dgoip->kdgoi', pwr, prod_re, precision=hp)
            - jnp.einsum('kdgp,dgoip->kdgoi', pwi, prod_im, precision=hp))
    kf, kb = kern[:, 0], kern[:, 1]
    kall = jnp.concatenate([kb[:0:-1], (kf[0] + kb[0])[None], kf[1:], jnp.zeros_like(kf[:1])], axis=0)
    m = _s5_toeplitz(jnp.transpose(kall, (1, 3, 2, 0)))

    def lane_pad(v):
        return jnp.pad(v, [(0, 0)] * (v.ndim - 1) + [(0, p)])

    def state_maps(exponents, d):
        pr, pi = power_per_step(exponents)
        pr, pi = pr[d][:, None], pi[d][:, None]
        br, bi = bbr_t[d][:, :, None, :], bbi_t[d][:, :, None, :]
        rows = lambda v: lane_pad(v).reshape(g, gc * t, 2 * p).astype(BF16)
        return rows(pr * br - pi * bi), rows(pr * bi + pi * br)

    f_slots = state_maps(t - 1.0 - steps, 0) + state_maps(steps, 1)

    def out_maps(exponents, d):
        pr, pi = power_per_step(exponents)
        pr = jnp.swapaxes(pr[d], -1, -2)[:, None]
        pi = jnp.swapaxes(pi[d], -1, -2)[:, None]
        c_r, c_i = cr[d][..., None], ci[d][..., None]
        cols = lambda v: jnp.pad(v, [(0, 0), (0, 0), (0, p), (0, 0)]).astype(BF16)
        return cols(c_r * pr - c_i * pi), cols(-(c_r * pi + c_i * pr))

    e_slots = out_maps(steps + 1.0, 0) + out_maps(t - steps, 1)
    pad = lambda v, axis: jnp.concatenate([v, jnp.zeros_like(v)], axis=axis)

    step_r, step_i = power(jnp.array([1.0, 2.0, 4.0, 8.0], F32)[:, None, None, None] * t)
    rows_a = []
    for d in range(2):
        for s in range(4):
            rows_a += [step_r[s, d], step_i[s, d]]
    r8 = jnp.arange(8, dtype=F32)[:, None, None, None] * t
    car_r, car_i = power(r8)
    rows = jnp.concatenate([jnp.stack(rows_a, axis=0), car_r[:, 0], car_i[:, 0],
                            car_r[::-1, 1], car_i[::-1, 1]], axis=0)
    tab = pad(jnp.transpose(rows, (1, 0, 2)), 2)
    return m, f_slots, e_slots, tab


def _s5_kernel(u_ref, m_ref, f0_ref, f1_ref, f2_ref, f3_ref, e0_ref, e1_ref, e2_ref, e3_ref,
               tab_ref, y_ref, s_ref, xin_ref):
    bsz, gc, n_c, t = u_ref.shape
    lanes = 2 * S5_STATE
    u = jnp.concatenate(
        [jnp.concatenate([u_ref[b, i] for i in range(gc)], axis=1) for b in range(bsz)], axis=0)
    f_all = jnp.concatenate([r[0] for r in (f0_ref, f1_ref, f2_ref, f3_ref)], axis=1)
    e_all = jnp.concatenate(
        [jnp.concatenate([r[0, o] for o in range(gc)], axis=1)
         for r in (e0_ref, e1_ref, e2_ref, e3_ref)], axis=0)
    s_ref[...] = _dot(u, f_all)

    row = lax.broadcasted_iota(jnp.int32, (8, lanes), 0)
    tab_a = tab_ref[0, 0:16, :]
    bc = lambda v: jnp.broadcast_to(v, (8, lanes))
    step_w = [[(bc(tab_a[8 * d + 2 * s:8 * d + 2 * s + 1]), bc(tab_a[8 * d + 2 * s + 1:8 * d + 2 * s + 2]))
               for s in range(4)] for d in range(2)]
    car = [(tab_ref[0, 16 + 16 * d:24 + 16 * d, :], tab_ref[0, 24 + 16 * d:32 + 16 * d, :])
           for d in range(2)]

    def shifted(v, sh, d):
        if d == 0:
            return jnp.where(row >= sh, pltpu.roll(v, sh, 0), 0.0)
        return jnp.where(row < 8 - sh, pltpu.roll(v, 8 - sh, 0), 0.0)

    def tile_scan(r0, hr, hi, d):
        c0 = 2 * lanes * d
        zr = s_ref[r0:r0 + 8, c0:c0 + lanes]
        zi = s_ref[r0:r0 + 8, c0 + lanes:c0 + 2 * lanes]
        for s, sh in enumerate((1, 2, 4)):
            wr, wi = step_w[d][s]
            sr, si = shifted(zr, sh, d), shifted(zi, sh, d)
            zr, zi = zr + wr * sr - wi * si, zi + wr * si + wi * sr
        pr, pi = car[d]
        xin_ref[r0:r0 + 8, c0:c0 + lanes] = pr * hr - pi * hi + shifted(zr, 1, d)
        xin_ref[r0:r0 + 8, c0 + lanes:c0 + 2 * lanes] = pr * hi + pi * hr + shifted(zi, 1, d)
        last = 0 if d else 7
        wr, wi = step_w[d][3]
        return (wr * hr - wi * hi + bc(zr[last:last + 1]), wr * hi + wi * hr + bc(zi[last:last + 1]))

    n_t = n_c // 8
    zero = jnp.zeros((8, lanes), F32)
    for b in range(bsz):
        fr, fi, br, bi = zero, zero, zero, zero
        for k in range(n_t):
            fr, fi = tile_scan(b * n_c + 8 * k, fr, fi, 0)
            br, bi = tile_scan(b * n_c + 8 * (n_t - 1 - k), br, bi, 1)

    y = _dot(u, m_ref[0]) + _dot(xin_ref[...].astype(BF16), e_all)
    for b in range(bsz):
        for o in range(gc):
            y_ref[b, o] = y[b * n_c:(b + 1) * n_c, o * t:(o + 1) * t].astype(BF16)


def _s5_scan(u_t, m, f_slots, e_slots, tab):
    bsz, width, seqlen = u_t.shape
    n_c = seqlen // S5_T
    gc = S5_GROUP_CH
    rows = bsz * n_c
    f_spec = pl.BlockSpec((1, gc * S5_T, 2 * S5_STATE), lambda q: (q, 0, 0))
    e_spec = pl.BlockSpec((1, gc, 2 * S5_STATE, S5_T), lambda q: (q, 0, 0, 0))
    y = pl.pallas_call(
        _s5_kernel,
        grid=(S5_GROUPS,),
        in_specs=[
            pl.BlockSpec((bsz, gc, n_c, S5_T), lambda q: (0, q, 0, 0)),
            pl.BlockSpec((1, gc * S5_T, gc * S5_T), lambda q: (q, 0, 0)),
            f_spec, f_spec, f_spec, f_spec, e_spec, e_spec, e_spec, e_spec,
            pl.BlockSpec((1, 48, 2 * S5_STATE), lambda q: (q, 0, 0)),
        ],
        out_specs=pl.BlockSpec((bsz, gc, n_c, S5_T), lambda q: (0, q, 0, 0)),
        out_shape=jax.ShapeDtypeStruct((bsz, width, n_c, S5_T), BF16),
        scratch_shapes=[pltpu.VMEM((rows, 8 * S5_STATE), F32), pltpu.VMEM((rows, 8 * S5_STATE), F32)],
        compiler_params=_params("parallel"),
    )(u_t.reshape(bsz, width, n_c, S5_T), m, *f_slots, *e_slots, tab)
    return y.reshape(bsz, width, seqlen)


def _gelu_tanh(v):
    return 0.5 * v * (1.0 + jnp.tanh(math.sqrt(2.0 / math.pi) * (v + 0.044715 * (v * v * v))))


def _outproj_kernel(yssd_ref, ys5_ref, u_ref, x_ref, g1_ref, sc2_ref, sh2_ref, d5_ref,
                    wglu_t_ref, bglu_ref, wout_ref, lng_ref, lnb_ref, x1_ref, h2_ref):
    y = ys5_ref[0].astype(F32) + u_ref[0].astype(F32) * d5_ref[...]
    y = _gelu_tanh(y)
    s5 = y * jax.nn.sigmoid(_dot(wglu_t_ref[...], y.astype(BF16)) + bglu_ref[...])
    mix = (_dot(yssd_ref[0], wout_ref[:SSD_WIDTH, :])
           + _dot_tn(s5.astype(BF16), wout_ref[SSD_WIDTH:, :]))
    x1 = _layer_norm(DEEPNORM_ALPHA * x_ref[0] + g1_ref[0] * mix, lng_ref[...], lnb_ref[...])
    x1_ref[0] = x1
    h2_ref[0] = (x1 * (1.0 + sc2_ref[0]) + sh2_ref[0]).astype(BF16)


def _outproj(yssd, ys5_t, u_t, x, g1, sc2, sh2, d5, wglu, bglu, wout, lng, lnb, tm):
    bsz, seqlen, _ = x.shape
    tok = lambda b, i: (b, i, 0)
    tok_t = lambda b, i: (b, 0, i)
    per_b = lambda b, i: (b, 0, 0)
    const = lambda b, i: (0, 0)
    row = lambda v: v.astype(F32).reshape(1, -1)
    col = lambda v: v.astype(F32).reshape(-1, 1)
    return pl.pallas_call(
        _outproj_kernel,
        grid=(bsz, seqlen // tm),
        in_specs=[
            pl.BlockSpec((1, tm, SSD_WIDTH), tok),
            pl.BlockSpec((1, S5_WIDTH, tm), tok_t),
            pl.BlockSpec((1, S5_WIDTH, tm), tok_t),
            pl.BlockSpec((1, tm, D_MODEL), tok),
            pl.BlockSpec((1, 1, D_MODEL), per_b),
            pl.BlockSpec((1, 1, D_MODEL), per_b),
            pl.BlockSpec((1, 1, D_MODEL), per_b),
            pl.BlockSpec((S5_WIDTH, 1), const),
            pl.BlockSpec((S5_WIDTH, S5_WIDTH), const),
            pl.BlockSpec((S5_WIDTH, 1), const),
            pl.BlockSpec((D_MODEL, D_MODEL), const),
            pl.BlockSpec((1, D_MODEL), const),
            pl.BlockSpec((1, D_MODEL), const),
        ],
        out_specs=[pl.BlockSpec((1, tm, D_MODEL), tok), pl.BlockSpec((1, tm, D_MODEL), tok)],
        out_shape=[jax.ShapeDtypeStruct((bsz, seqlen, D_MODEL), F32),
                   jax.ShapeDtypeStruct((bsz, seqlen, D_MODEL), BF16)],
        compiler_params=_params("parallel", "parallel"),
    )(yssd, ys5_t, u_t, x, g1, sc2, sh2, col(d5), wglu.T.astype(BF16), col(bglu), wout.astype(BF16),
      row(lng), row(lnb))


def _round_up(v, m):
    return (v + m - 1) // m * m


def _route_subtile(h2_s, wr_t, br, tri):
    s_len = h2_s.shape[0]
    logits = _dot_nt(wr_t, h2_s) + br
    mx = jnp.max(logits, axis=0, keepdims=True)
    ex = jnp.exp(logits - mx)
    probs = ex / jnp.sum(ex, axis=0, keepdims=True)
    row = lax.broadcasted_iota(jnp.int32, (N_EXPERTS, s_len), 0)
    n_grp = N_EXPERTS // EXPERTS_PER_GROUP
    gs = [jnp.max(probs[EXPERTS_PER_GROUP * g:EXPERTS_PER_GROUP * (g + 1)], axis=0, keepdims=True)
          for g in range(n_grp)]
    gmax = functools.reduce(jnp.maximum, gs)
    sel_grp = jnp.full((1, s_len), n_grp - 1, jnp.int32)
    for g in range(n_grp - 2, -1, -1):
        sel_grp = jnp.where(gs[g] == gmax, g, sel_grp)
    in_grp = (row // EXPERTS_PER_GROUP) == sel_grp
    masked = jnp.where(in_grp, probs, -1.0)
    m1 = jnp.max(masked, axis=0, keepdims=True)
    i1 = jnp.min(jnp.where(masked == m1, row, N_EXPERTS), axis=0, keepdims=True)
    is1 = row == i1
    masked2 = jnp.where(is1, -2.0, masked)
    m2 = jnp.max(masked2, axis=0, keepdims=True)
    i2 = jnp.min(jnp.where(masked2 == m2, row, N_EXPERTS), axis=0, keepdims=True)
    is2 = row == i2
    den = m1 + m2
    sel = jnp.where(is1, 1.0, jnp.where(is2, 1.0, 0.0))
    pos = _dot(sel.astype(BF16), tri)
    return is1, is2, m1 / den, m2 / den, sel, pos


def _copy_rows(src_ref, src0, dst_ref, dst0, n_chunks):
    def body(c, carry):
        so = pl.multiple_of(src0 + c * MOE_ALIGN, MOE_ALIGN)
        do = pl.multiple_of(dst0 + c * MOE_ALIGN, MOE_ALIGN)
        dst_ref[pl.ds(do, MOE_ALIGN), :] = src_ref[pl.ds(so, MOE_ALIGN), :]
        return carry
    lax.fori_loop(0, n_chunks, body, 0)


def _zero_rows(dst_ref, dst0, n_chunks):
    def body(c, carry):
        do = pl.multiple_of(dst0 + c * MOE_ALIGN, MOE_ALIGN)
        dst_ref[pl.ds(do, MOE_ALIGN), :] = jnp.zeros((MOE_ALIGN, dst_ref.shape[1]), dst_ref.dtype)
        return carry
    lax.fori_loop(0, n_chunks, body, 0)


def _moe_kernel(h2_ref, x1_ref, g2_ref, wr_t_ref, br_ref, tri_ref, wg_ref, wu_ref, wd_ref,
                lng_ref, lnb_ref, out_ref, xg_ref, cb_ref, route_ref, meta_ref):
    step = pl.program_id(2)
    tm = h2_ref.shape[1]
    s_len = x1_ref.shape[1]
    n_sub = tm // s_len
    cb_rows = cb_ref.shape[0]
    ne = N_EXPERTS
    seg_len, seg_cb, seg_xg, e_base, e_blk, sub_rows = 0, 64, 128, 192, 208, 224

    def onehot_rows(r0, n_rows, t1, t2, v1, v2):
        slot = (lax.broadcasted_iota(jnp.int32, (n_rows, s_len), 0) + r0).astype(F32)
        return jnp.where(slot == t1, v1, jnp.where(slot == t2, v2, 0.0))

    @pl.when(step == 0)
    def _dispatch():
        erow = lax.broadcasted_iota(jnp.int32, (ne, 1), 0)
        for s in range(n_sub):
            h2_s = h2_ref[0, s * s_len:(s + 1) * s_len, :]
            is1, is2, g1, g2, sel, pos = _route_subtile(h2_s, wr_t_ref[...], br_ref[...], tri_ref[...])
            base_vec = jnp.zeros((ne, 1), F32)
            run = jnp.int32(0)
            for e in range(ne):
                n16 = _round_up(jnp.sum(sel[e:e + 1, :]).astype(jnp.int32), MOE_ALIGN)
                meta_ref[seg_len + s * ne + e] = n16
                meta_ref[seg_cb + s * ne + e] = run
                base_vec = jnp.where(erow == e, run.astype(F32), base_vec)
                run = run + n16
            meta_ref[sub_rows + s] = run
            tgt = base_vec + pos
            route_ref[s, 0:1, :] = jnp.sum(jnp.where(is1, tgt, 0.0), axis=0, keepdims=True)
            route_ref[s, 1:2, :] = jnp.sum(jnp.where(is2, tgt, 0.0), axis=0, keepdims=True)
            route_ref[s, 2:3, :] = g1
            route_ref[s, 3:4, :] = g2
        run = jnp.int32(0)
        for e in range(ne):
            rows_e = jnp.int32(0)
            for s in range(n_sub):
                meta_ref[seg_xg + s * ne + e] = run + rows_e
                rows_e = rows_e + meta_ref[seg_len + s * ne + e]
            blocks = (rows_e + MOE_ROWS - 1) // MOE_ROWS
            meta_ref[e_base + e] = run
            meta_ref[e_blk + e] = blocks
            _zero_rows(xg_ref, run + rows_e, (blocks * MOE_ROWS - rows_e) // MOE_ALIGN)
            run = run + blocks * MOE_ROWS
        for s in range(n_sub):
            h2_s = h2_ref[0, s * s_len:(s + 1) * s_len, :]
            t1, t2 = route_ref[s, 0:1, :], route_ref[s, 1:2, :]
            for r0 in range(0, cb_rows, MOE_GATHER_ROWS):
                onehot = onehot_rows(r0, MOE_GATHER_ROWS, t1, t2, 1.0, 1.0).astype(BF16)
                cb_ref[r0:r0 + MOE_GATHER_ROWS, :] = _dot(onehot, h2_s).astype(BF16)
            for e in range(ne):
                _copy_rows(cb_ref, meta_ref[seg_cb + s * ne + e], xg_ref, meta_ref[seg_xg + s * ne + e],
                           meta_ref[seg_len + s * ne + e] // MOE_ALIGN)

    n_expert_steps = ne // MOE_EXPERTS_PER_STEP

    def run_expert(k):
        e = step * MOE_EXPERTS_PER_STEP + k
        base = meta_ref[e_base + e]
        n_blk = meta_ref[e_blk + e]

        def run_rows(r0, rows):
            xe = xg_ref[pl.ds(r0, rows), :]
            hid = (_silu(_dot(xe, wg_ref[k])) * _dot(xe, wu_ref[k])).astype(BF16)
            xg_ref[pl.ds(r0, rows), :] = _dot(hid, wd_ref[k]).astype(BF16)

        def pair(j, carry):
            run_rows(pl.multiple_of(base + j * 2 * MOE_ROWS, MOE_ROWS), 2 * MOE_ROWS)
            return carry

        lax.fori_loop(0, n_blk // 2, pair, 0)

        @pl.when(n_blk % 2 == 1)
        def _tail():
            run_rows(pl.multiple_of(base + (n_blk - 1) * MOE_ROWS, MOE_ROWS), MOE_ROWS)

    @pl.when(step < n_expert_steps)
    def _experts():
        for k in range(MOE_EXPERTS_PER_STEP):
            run_expert(k)

    @pl.when(step >= n_expert_steps)
    def _combine():
        s = step - n_expert_steps
        for e in range(ne):
            _copy_rows(xg_ref, meta_ref[seg_xg + s * ne + e], cb_ref, meta_ref[seg_cb + s * ne + e],
                       meta_ref[seg_len + s * ne + e] // MOE_ALIGN)
        used = meta_ref[sub_rows + s]
        _zero_rows(cb_ref, used, (cb_rows - used) // MOE_ALIGN)
        t1, t2 = route_ref[s, 0:1, :], route_ref[s, 1:2, :]
        g1, g2 = route_ref[s, 2:3, :], route_ref[s, 3:4, :]
        acc = jnp.zeros((s_len, D_MODEL), F32)
        for r0 in range(0, cb_rows, MOE_GATHER_ROWS):
            weighted = onehot_rows(r0, MOE_GATHER_ROWS, t1, t2, g1, g2).astype(BF16)
            acc = acc + _dot_tn(weighted, cb_ref[r0:r0 + MOE_GATHER_ROWS, :])
        v = DEEPNORM_ALPHA * x1_ref[0] + g2_ref[0] * acc
        out_ref[0] = _layer_norm(v, lng_ref[...], lnb_ref[...])


def _moe(h2, x1, g2, wr_t, br, tri, wg, wu, wd, lng, lnb, tm, s_len):
    bsz, seqlen, _ = x1.shape
    n_sub = tm // s_len
    ne = N_EXPERTS
    cb_rows = _round_up(2 * s_len + ne * (MOE_ALIGN - 1), MOE_GATHER_ROWS)
    xg_rows = _round_up(2 * tm + n_sub * ne * (MOE_ALIGN - 1) + ne * (MOE_ROWS - 1), MOE_ROWS)
    tok = lambda b, i, st: (b, i, 0)
    n_es = ne // MOE_EXPERTS_PER_STEP
    sub = lambda b, i, st: (b, i * n_sub + jnp.clip(st - n_es, 0, n_sub - 1), 0)
    per_b = lambda b, i, st: (b, 0, 0)
    const = lambda b, i, st: (0, 0)
    per_e = lambda b, i, st: (jnp.minimum(st, n_es - 1), 0, 0)
    row = lambda v: v.astype(F32).reshape(1, -1)
    return pl.pallas_call(
        _moe_kernel,
        grid=(bsz, seqlen // tm, n_es + n_sub),
        in_specs=[
            pl.BlockSpec((1, tm, D_MODEL), tok),
            pl.BlockSpec((1, s_len, D_MODEL), sub),
            pl.BlockSpec((1, 1, D_MODEL), per_b),
            pl.BlockSpec((ne, D_MODEL), const),
            pl.BlockSpec((ne, 1), const),
            pl.BlockSpec((s_len, s_len), const),
            pl.BlockSpec((MOE_EXPERTS_PER_STEP, D_MODEL, D_EXPERT), per_e),
            pl.BlockSpec((MOE_EXPERTS_PER_STEP, D_MODEL, D_EXPERT), per_e),
            pl.BlockSpec((MOE_EXPERTS_PER_STEP, D_EXPERT, D_MODEL), per_e),
            pl.BlockSpec((1, D_MODEL), const),
            pl.BlockSpec((1, D_MODEL), const),
        ],
        out_specs=pl.BlockSpec((1, s_len, D_MODEL), sub),
        out_shape=jax.ShapeDtypeStruct((bsz, seqlen, D_MODEL), F32),
        scratch_shapes=[pltpu.VMEM((xg_rows, D_MODEL), BF16), pltpu.VMEM((cb_rows, D_MODEL), BF16),
                        pltpu.VMEM((n_sub, 8, s_len), F32), pltpu.SMEM((256,), jnp.int32)],
        compiler_params=_params("parallel", "parallel", "arbitrary"),
    )(h2, x1, g2, wr_t, br, tri, wg, wu, wd, row(lng), row(lnb))


def _layer_weights(i, w):
    w_in = w['w_in'][i]
    n_dt = 2 * SSD_HEADS
    dt_lo = SSD_WIDTH + SSD_CONV_CH
    w_dt = w_in[:, dt_lo:dt_lo + n_dt]
    w_u = w_in[:, dt_lo + n_dt:]
    w_all = jnp.concatenate(
        [w_in[:, :dt_lo], w_dt, jnp.zeros((D_MODEL, DT_PAD - n_dt), w_in.dtype)], axis=1).astype(BF16)
    w_t = jnp.concatenate([w_u, w_dt], axis=1).T.astype(BF16)
    m, f, e, tab = _s5_tables(w['s5_lam_re'][i], w['s5_lam_im'][i], w['s5_log_step'][i],
                              w['s5_b_re'][i], w['s5_b_im'][i], w['s5_c_re'][i], w['s5_c_im'][i])
    return dict(
        w_all=w_all, w_t=w_t,
        s5=(m, f, e, tab),
        wg=w['w_gate'][i].astype(BF16), wu=w['w_up'][i].astype(BF16), wd=w['w_down'][i].astype(BF16),
    )


def _trunk(x, mods, w, layer_w, tiles):
    bsz, seqlen, _ = x.shape
    tm, tq, tmoe = tiles
    wr_t = w['w_router'].T.astype(BF16)
    br = w['b_router'].astype(F32).reshape(N_EXPERTS, 1)
    s_len = min(MOE_SUB, tmoe)
    ids = jnp.arange(s_len)
    tri = (ids[:, None] < ids[None, :]).astype(BF16)
    for i in range(DEPTH):
        lw = layer_w[i]
        sh1, sc1, g1, sh2, sc2, g2 = [v.reshape(bsz, 1, D_MODEL) for v in jnp.split(mods[i], 6, axis=-1)]
        z, xbc, dt, u_t, dt_t = _inproj(x, sc1, sh1, lw['w_all'], lw['w_t'],
                                        w['conv_w'][i], w['conv_b'][i], tm)
        y_f = _ssd_dir(xbc, dt, dt_t, w['ssd_a_log'][i], w['ssd_dt_bias'][i], False, tq)
        y_ssd = _ssd_dir(xbc, dt, dt_t, w['ssd_a_log'][i], w['ssd_dt_bias'][i], True, tq,
                         z=z, yf=y_f, d_skip=w['ssd_d'][i], norm_w=w['ssd_norm_w'][i])
        y_s5_t = _s5_scan(u_t, *lw['s5'])
        x1, h2 = _outproj(y_ssd, y_s5_t, u_t, x, g1, sc2, sh2, w['s5_d'][i], w['s5_w_glu'][i],
                          w['s5_b_glu'][i], w['w_out'][i], w['ln1_g'][i], w['ln1_b'][i], tm)
        x = _moe(h2, x1, g2, wr_t, br, tri, lw['wg'], lw['wu'], lw['wd'],
                 w['ln2_g'][i], w['ln2_b'][i], tmoe, s_len)
    return x


def _pick(n, pref):
    return pref if n % pref == 0 else n


def _run(x_groups, c_groups, w):
    rows = sum(c.shape[0] for c in c_groups)
    pad = (-rows) % 8
    c_all = jnp.concatenate(list(c_groups) + [jnp.zeros((pad, D_MODEL), F32)], axis=0).astype(F32)
    mods = _ada_mod(c_all, w['w_ada'].astype(F32), w['b_ada'].astype(F32))
    layer_w = [_layer_weights(i, w) for i in range(DEPTH)]
    outs = []
    r0 = 0
    for x, c in zip(x_groups, c_groups):
        bsz, seqlen, _ = x.shape
        assert seqlen % (8 * S5_T) == 0, "sequence length must be a multiple of 8 S5 chunks"
        tiles = (_pick(seqlen, 1024), _pick(seqlen, 1024), _pick(seqlen, 2048))
        outs.append(_trunk(x, mods[:, r0:r0 + bsz], w, layer_w, tiles))
        r0 += bsz
    return tuple(outs)


def kernel(x_prompt, x_sample, c_prompt, c_sample, w_ada, b_ada, w_in, conv_w, conv_b, ssd_a_log, ssd_dt_bias, ssd_d, ssd_norm_w, s5_lam_re, s5_lam_im, s5_log_step, s5_b_re, s5_b_im, s5_c_re, s5_c_im, s5_d, s5_w_glu, s5_b_glu, w_out, ln1_g, ln1_b, ln2_g, ln2_b, w_router, b_router, w_gate, w_up, w_down):
    w = dict(w_ada=w_ada, b_ada=b_ada, w_in=w_in, conv_w=conv_w, conv_b=conv_b, ssd_a_log=ssd_a_log,
             ssd_dt_bias=ssd_dt_bias, ssd_d=ssd_d, ssd_norm_w=ssd_norm_w, s5_lam_re=s5_lam_re,
             s5_lam_im=s5_lam_im, s5_log_step=s5_log_step, s5_b_re=s5_b_re, s5_b_im=s5_b_im,
             s5_c_re=s5_c_re, s5_c_im=s5_c_im, s5_d=s5_d, s5_w_glu=s5_w_glu, s5_b_glu=s5_b_glu,
             w_out=w_out, ln1_g=ln1_g, ln1_b=ln1_b, ln2_g=ln2_g, ln2_b=ln2_b, w_router=w_router,
             b_router=b_router, w_gate=w_gate, w_up=w_up, w_down=w_down)
    return _run((x_prompt, x_sample), (c_prompt, c_sample), w)
```

```python
import functools
import math

import jax
import jax.numpy as jnp
from jax import lax
from jax.experimental import pallas as pl
from jax.experimental.pallas import tpu as pltpu

F32 = jnp.float32
BF16 = jnp.bfloat16

D_MODEL = 1024
DEPTH = 4
SSD_WIDTH = 512
SSD_HEADDIM = 64
SSD_HEADS = 8
SSD_GROUPS = 2
SSD_STATE = 128
SSD_CONV = 5
SSD_CONV_CH = SSD_WIDTH + 2 * SSD_GROUPS * SSD_STATE
S5_WIDTH = 512
S5_GROUP_CH = 16
S5_GROUPS = 32
S5_STATE = 64
S5_MIN_DECAY = 1e-4
N_EXPERTS = 16
EXPERTS_PER_GROUP = 4
D_EXPERT = 512
DEEPNORM_ALPHA = (2 * DEPTH) ** 0.25
LN_EPS = 1e-5
RMS_EPS = 1e-5

SSD_Q = 128
S5_T = 128
PROJ_COLS = SSD_WIDTH + SSD_CONV_CH
DT_PAD = 128
CONV_HALO = 16
MOE_ROWS = 128
MOE_ALIGN = 16
MOE_GATHER_ROWS = 256
MOE_SUB = 512
MOE_EXPERTS_PER_STEP = 2
VMEM_LIMIT = 56 * 1024 * 1024


def _dot(a, b):
    return jnp.dot(a, b, preferred_element_type=F32)


def _dot_nt(a, b):
    return lax.dot_general(a, b, (((1,), (1,)), ((), ())), preferred_element_type=F32)


def _dot_tn(a, b):
    return lax.dot_general(a, b, (((0,), (0,)), ((), ())), preferred_element_type=F32)


def _split_bf16(v):
    hi = v.astype(BF16)
    lo = (v - hi.astype(F32)).astype(BF16)
    return hi, lo


def _silu(v):
    return v * jax.nn.sigmoid(v)


def _layer_norm(v, g, b):
    mu = jnp.mean(v, axis=-1, keepdims=True)
    vc = v - mu
    var = jnp.mean(vc * vc, axis=-1, keepdims=True)
    return vc * lax.rsqrt(var + LN_EPS) * g + b


def _params(*sem):
    return pltpu.CompilerParams(dimension_semantics=sem, vmem_limit_bytes=VMEM_LIMIT)


def _ada_kernel(c_ref, w_ref, b_ref, o_ref):
    ca = _silu(c_ref[...])
    c_hi, c_lo = _split_bf16(ca)
    w_hi, w_lo = _split_bf16(w_ref[0])
    acc = _dot(c_hi, w_hi) + _dot(c_lo, w_hi) + _dot(c_hi, w_lo)
    o_ref[0] = acc + b_ref[0]


def _ada_mod(c_all, w_ada, b_ada):
    rows = c_all.shape[0]
    n_out = w_ada.shape[-1]
    bn = 1536
    return pl.pallas_call(
        _ada_kernel,
        grid=(DEPTH, n_out // bn),
        in_specs=[
            pl.BlockSpec((rows, D_MODEL), lambda l, j: (0, 0)),
            pl.BlockSpec((1, D_MODEL, bn), lambda l, j: (l, 0, j)),
            pl.BlockSpec((1, 1, bn), lambda l, j: (l, 0, j)),
        ],
        out_specs=pl.BlockSpec((1, rows, bn), lambda l, j: (l, 0, j)),
        out_shape=jax.ShapeDtypeStruct((DEPTH, rows, n_out), F32),
        compiler_params=_params("arbitrary", "arbitrary"),
    )(c_all, w_ada, b_ada.reshape(DEPTH, 1, n_out))


def _inproj_kernel(x_ref, xl_ref, xr_ref, sc_ref, sh_ref, w_ref, w_t_ref, cw_ref, cb_ref,
                   z_ref, xbc_ref, dt_ref, u_t_ref, dt_t_ref, ext_ref):
    i = pl.program_id(1)
    n = pl.num_programs(1)
    tm = x_ref.shape[1]
    scale = 1.0 + sc_ref[0]
    shift = sh_ref[0]
    h = (x_ref[0] * scale + shift).astype(BF16)
    p = _dot(h, w_ref[...])
    z_ref[0] = p[:, :SSD_WIDTH].astype(BF16)
    dt_ref[0] = p[:, PROJ_COLS:PROJ_COLS + 2 * SSD_HEADS]
    p_t = _dot_nt(w_t_ref[...], h)
    u_t_ref[0] = p_t[:S5_WIDTH].astype(BF16)
    dt_t_ref[0] = p_t[S5_WIDTH:]

    halo = jnp.concatenate([xl_ref[0], xr_ref[0]], axis=0)
    p_halo = _dot((halo * scale + shift).astype(BF16), w_ref[:, SSD_WIDTH:PROJ_COLS])
    has_left = (i > 0).astype(F32)
    has_right = (i < n - 1).astype(F32)
    ext_ref[0:CONV_HALO, :] = p_halo[:CONV_HALO] * has_left
    ext_ref[CONV_HALO:CONV_HALO + tm, :] = p[:, SSD_WIDTH:PROJ_COLS]
    ext_ref[CONV_HALO + tm:2 * CONV_HALO + tm, :] = p_halo[CONV_HALO:] * has_right
    first = CONV_HALO - SSD_CONV // 2
    acc = cb_ref[...] + ext_ref[first:first + tm, :] * cw_ref[0:1, :]
    for k in range(1, SSD_CONV):
        acc = acc + ext_ref[first + k:first + k + tm, :] * cw_ref[k:k + 1, :]
    xbc_ref[0] = _silu(acc).astype(BF16)


def _inproj(x, sc, sh, w_all, w_t, conv_w, conv_b, tm):
    bsz, seqlen, _ = x.shape
    per_blk = tm // CONV_HALO
    n_halo = seqlen // CONV_HALO
    tok = lambda b, i: (b, i, 0)
    tok_t = lambda b, i: (b, 0, i)
    per_b = lambda b, i: (b, 0, 0)
    const = lambda b, i: (0, 0)
    w_pad = jnp.zeros((8, SSD_CONV_CH), F32).at[:SSD_CONV].set(conv_w.astype(F32))
    return pl.pallas_call(
        _inproj_kernel,
        grid=(bsz, seqlen // tm),
        in_specs=[
            pl.BlockSpec((1, tm, D_MODEL), tok),
            pl.BlockSpec((1, CONV_HALO, D_MODEL), lambda b, i: (b, jnp.maximum(i * per_blk - 1, 0), 0)),
            pl.BlockSpec((1, CONV_HALO, D_MODEL),
                         lambda b, i: (b, jnp.minimum((i + 1) * per_blk, n_halo - 1), 0)),
            pl.BlockSpec((1, 1, D_MODEL), per_b),
            pl.BlockSpec((1, 1, D_MODEL), per_b),
            pl.BlockSpec(w_all.shape, const),
            pl.BlockSpec(w_t.shape, const),
            pl.BlockSpec((8, SSD_CONV_CH), const),
            pl.BlockSpec((1, SSD_CONV_CH), const),
        ],
        out_specs=[
            pl.BlockSpec((1, tm, SSD_WIDTH), tok),
            pl.BlockSpec((1, tm, SSD_CONV_CH), tok),
            pl.BlockSpec((1, tm, 2 * SSD_HEADS), tok),
            pl.BlockSpec((1, S5_WIDTH, tm), tok_t),
            pl.BlockSpec((1, 2 * SSD_HEADS, tm), tok_t),
        ],
        out_shape=[
            jax.ShapeDtypeStruct((bsz, seqlen, SSD_WIDTH), BF16),
            jax.ShapeDtypeStruct((bsz, seqlen, SSD_CONV_CH), BF16),
            jax.ShapeDtypeStruct((bsz, seqlen, 2 * SSD_HEADS), F32),
            jax.ShapeDtypeStruct((bsz, S5_WIDTH, seqlen), BF16),
            jax.ShapeDtypeStruct((bsz, 2 * SSD_HEADS, seqlen), F32),
        ],
        scratch_shapes=[pltpu.VMEM((tm + 2 * CONV_HALO, SSD_CONV_CH), F32)],
        compiler_params=_params("parallel", "parallel"),
    )(x, x, x, sc, sh, w_all, w_t, w_pad, conv_b.astype(F32).reshape(1, SSD_CONV_CH))


def _softplus(v):
    return jnp.maximum(v, 0.0) + jnp.log1p(jnp.exp(-jnp.abs(v)))


def _head_indicator(width):
    head = lax.broadcasted_iota(jnp.int32, (SSD_HEADS, SSD_HEADS * width), 0)
    lane = lax.broadcasted_iota(jnp.int32, (SSD_HEADS, SSD_HEADS * width), 1)
    return jnp.where(lane // width == head, 1.0, 0.0).astype(BF16)


def _ssd_chunk(xbc, dt_raw, dt_raw_t, a_row, a_col, bias_row, bias_col, state_ref, rev):
    q = xbc.shape[0]
    x = xbc[:, :SSD_WIDTH]
    b_mat = xbc[:, SSD_WIDTH:SSD_WIDTH + SSD_GROUPS * SSD_STATE]
    c_mat = xbc[:, SSD_WIDTH + SSD_GROUPS * SSD_STATE:]
    dts = _softplus(dt_raw + bias_row)
    dts_t = _softplus(dt_raw_t + bias_col)
    dta = dts * a_row
    dta_t = dts_t * a_col

    row = lax.broadcasted_iota(jnp.int32, (q, q), 0)
    col = lax.broadcasted_iota(jnp.int32, (q, q), 1)
    lower = col <= row
    upper = col >= row
    tri_lower = jnp.where(lower, 1.0, 0.0).astype(BF16)
    tri_upper = jnp.where(upper, 1.0, 0.0).astype(BF16)
    tri_c, tri_r, mask = (tri_upper, tri_lower, upper) if rev else (tri_lower, tri_upper, lower)

    d_hi, d_lo = _split_bf16(dta)
    acc = _dot(tri_c, d_hi) + _dot(tri_c, d_lo)
    t_hi, t_lo = _split_bf16(dta_t)
    acc_t = _dot(t_hi, tri_r) + _dot(t_lo, tri_r)

    lane128 = lax.broadcasted_iota(jnp.int32, (q, 128), 1)
    x32 = x.astype(F32)
    cb = [_dot_nt(c_mat[:, g * SSD_STATE:(g + 1) * SSD_STATE],
                  b_mat[:, g * SSD_STATE:(g + 1) * SSD_STATE]) for g in range(SSD_GROUPS)]
    a_hi, a_lo = _split_bf16(acc)
    ind_q = _head_indicator(q)
    t_row = jnp.concatenate([acc_t[h:h + 1, :] for h in range(SSD_HEADS)], axis=1)
    seg_all = _dot(a_hi, ind_q) + _dot(a_lo, ind_q) - t_row
    ys = []
    for k in range(SSD_HEADS // 2):
        m_pair = []
        for hh in range(2):
            h = 2 * k + hh
            g = h // (SSD_HEADS // SSD_GROUPS)
            seg = seg_all[:, h * q:(h + 1) * q]
            dec = jnp.where(mask, jnp.exp(jnp.minimum(seg, 0.0)), 0.0)
            m_pair.append((cb[g] * dec * dts_t[h:h + 1, :]).astype(BF16))
        xp = x32[:, 128 * k:128 * (k + 1)]
        x_blk = jnp.concatenate([jnp.where(lane128 < SSD_HEADDIM, xp, 0.0).astype(BF16),
                                 jnp.where(lane128 < SSD_HEADDIM, 0.0, xp).astype(BF16)], axis=0)
        ys.append(_dot(jnp.concatenate(m_pair, axis=1), x_blk))
    y = jnp.concatenate(ys, axis=1)

    end = acc[0:1, :] if rev else acc[q - 1:q, :]
    ind_p = _head_indicator(SSD_HEADDIM)
    y_scale = _dot(jnp.exp(acc).astype(BF16), ind_p)
    x_w = _dot((dts * jnp.exp(end - acc)).astype(BF16), ind_p)
    c_hi, c_lo = _split_bf16(jnp.broadcast_to(jnp.exp(end), (8, SSD_HEADS)))
    carry = (_dot(c_hi, ind_p) + _dot(c_lo, ind_p))[0:1, :]
    xw = (x32 * x_w).astype(BF16)
    half = SSD_WIDTH // SSD_GROUPS
    y_off = []
    for g in range(SSD_GROUPS):
        st = state_ref[g]
        y_off.append(_dot(c_mat[:, g * SSD_STATE:(g + 1) * SSD_STATE], st.astype(BF16)))
        state_ref[g] = st * carry[:, g * half:(g + 1) * half] + _dot_tn(
            b_mat[:, g * SSD_STATE:(g + 1) * SSD_STATE], xw[:, g * half:(g + 1) * half])
    return y + jnp.concatenate(y_off, axis=1) * y_scale


def _ssd_dir_kernel(*refs, rev):
    if rev:
        (xbc_ref, dt_ref, dt_t_ref, alog_ref, alog_t_ref, bias_ref, bias_t_ref,
         z_ref, yf_ref, dskip_ref, nw_ref, o_ref, state_ref) = refs
    else:
        (xbc_ref, dt_ref, dt_t_ref, alog_ref, alog_t_ref, bias_ref, bias_t_ref,
         o_ref, state_ref) = refs

    @pl.when(pl.program_id(1) == 0)
    def _():
        state_ref[...] = jnp.zeros(state_ref.shape, F32)

    d = 1 if rev else 0
    hs = slice(d * SSD_HEADS, (d + 1) * SSD_HEADS)
    a_row = -jnp.exp(alog_ref[:, hs])
    a_col = -jnp.exp(alog_t_ref[hs, :])
    bias_row = bias_ref[:, hs]
    bias_col = bias_t_ref[hs, :]
    tq = xbc_ref.shape[1]
    n_chunks = tq // SSD_Q
    order = range(n_chunks - 1, -1, -1) if rev else range(n_chunks)
    for ci in order:
        rs = slice(ci * SSD_Q, (ci + 1) * SSD_Q)
        xbc = xbc_ref[0, rs, :]
        y = _ssd_chunk(xbc, dt_ref[0, rs, hs], dt_t_ref[0, hs, rs], a_row, a_col,
                       bias_row, bias_col, state_ref, rev)
        if rev:
            y = y + yf_ref[0, rs, :].astype(F32) + xbc[:, :SSD_WIDTH].astype(F32) * dskip_ref[...]
            y = y * _silu(z_ref[0, rs, :].astype(F32))
            y = y * lax.rsqrt(jnp.mean(y * y, axis=-1, keepdims=True) + RMS_EPS) * nw_ref[...]
            o_ref[0, rs, :] = y.astype(BF16)
        else:
            o_ref[0, rs, :] = y.astype(BF16)


def _ssd_dir(xbc, dt, dt_t, a_log, dt_bias, rev, tq, z=None, yf=None, d_skip=None, norm_w=None):
    bsz, seqlen, _ = xbc.shape
    nblk = seqlen // tq
    if rev:
        tok = lambda b, i: (b, nblk - 1 - i, 0)
        tok_t = lambda b, i: (b, 0, nblk - 1 - i)
    else:
        tok = lambda b, i: (b, i, 0)
        tok_t = lambda b, i: (b, 0, i)
    const = lambda b, i: (0, 0)
    nh2 = 2 * SSD_HEADS
    a_flat = a_log.astype(F32).reshape(1, nh2)
    bias_flat = dt_bias.astype(F32).reshape(1, nh2)
    args = [xbc, dt, dt_t, a_flat, a_flat.reshape(nh2, 1), bias_flat, bias_flat.reshape(nh2, 1)]
    in_specs = [
        pl.BlockSpec((1, tq, SSD_CONV_CH), tok),
        pl.BlockSpec((1, tq, nh2), tok),
        pl.BlockSpec((1, nh2, tq), tok_t),
        pl.BlockSpec((1, nh2), const),
        pl.BlockSpec((nh2, 1), const),
        pl.BlockSpec((1, nh2), const),
        pl.BlockSpec((nh2, 1), const),
    ]
    if rev:
        args += [z, yf, jnp.repeat(d_skip.astype(F32), SSD_HEADDIM).reshape(1, SSD_WIDTH),
                 norm_w.astype(F32).reshape(1, SSD_WIDTH)]
        in_specs += [
            pl.BlockSpec((1, tq, SSD_WIDTH), tok),
            pl.BlockSpec((1, tq, SSD_WIDTH), tok),
            pl.BlockSpec((1, SSD_WIDTH), const),
            pl.BlockSpec((1, SSD_WIDTH), const),
        ]
    return pl.pallas_call(
        functools.partial(_ssd_dir_kernel, rev=rev),
        grid=(bsz, nblk),
        in_specs=in_specs,
        out_specs=pl.BlockSpec((1, tq, SSD_WIDTH), tok),
        out_shape=jax.ShapeDtypeStruct((bsz, seqlen, SSD_WIDTH), BF16),
        scratch_shapes=[pltpu.VMEM((SSD_GROUPS, SSD_STATE, SSD_WIDTH // SSD_GROUPS), F32)],
        compiler_params=_params("parallel", "arbitrary"),
    )(*args)


def _s5_toeplitz_kernel(k_ref, m_ref):
    t = S5_T
    gc = S5_GROUP_CH
    def rows_of_channel(i, carry):
        r0 = pl.multiple_of(i * t, t)
        for o in range(gc):
            taps = jnp.broadcast_to(k_ref[0, i, o:o + 1, :], (t, 2 * t))
            skew = pltpu.roll(taps, t + 1, 1, stride=1, stride_axis=0)
            m_ref[0, pl.ds(r0, t), o * t:(o + 1) * t] = skew[:, :t].astype(BF16)
        return carry

    lax.fori_loop(0, gc, rows_of_channel, 0)


def _s5_toeplitz(kall):
    g, gc, _, taps = kall.shape
    n = gc * S5_T
    return pl.pallas_call(
        _s5_toeplitz_kernel,
        grid=(g,),
        in_specs=[pl.BlockSpec((1, gc, gc, taps), lambda q: (q, 0, 0, 0))],
        out_specs=pl.BlockSpec((1, n, n), lambda q: (q, 0, 0)),
        out_shape=jax.ShapeDtypeStruct((g, n, n), BF16),
        compiler_params=_params("parallel"),
    )(kall)


def _s5_tables(lam_re, lam_im, log_step, b_re, b_im, c_re, c_im):
    hp = lax.Precision.HIGHEST
    t = S5_T
    g, p, gc = S5_GROUPS, S5_STATE, S5_GROUP_CH
    lr = jnp.minimum(lam_re.astype(F32), -S5_MIN_DECAY)
    li = lam_im.astype(F32)
    step = jnp.exp(log_step.astype(F32))[..., None]
    ar, ai = lr * step, li * step

    def power(n):
        mag = jnp.exp(n * ar)
        return mag * jnp.cos(n * ai), mag * jnp.sin(n * ai)

    def lane_pad(v):
        return jnp.pad(v, [(0, 0)] * (v.ndim - 1) + [(0, p)])

    def power_per_step(n):
        arg_r = n[:, None] * lane_pad(ar)[:, :, None, :]
        arg_i = n[:, None] * lane_pad(ai)[:, :, None, :]
        mag = jnp.exp(arg_r)
        return mag * jnp.cos(arg_i), mag * jnp.sin(arg_i)

    lbr, lbi = power(1.0)
    den = lr * lr + li * li
    qr = ((lbr - 1.0) * lr + lbi * li) / den
    qi = (lbi * lr - (lbr - 1.0) * li) / den
    bre, bim = b_re.astype(F32)[None], b_im.astype(F32)[None]
    bbr = qr[..., None] * bre - qi[..., None] * bim
    bbi = qr[..., None] * bim + qi[..., None] * bre
    cr, ci = c_re.astype(F32), c_im.astype(F32)

    steps = jnp.arange(t, dtype=F32)
    pwr, pwi = power(steps[:, None, None, None])
    bbr_t, bbi_t = jnp.swapaxes(bbr, -1, -2), jnp.swapaxes(bbi, -1, -2)
    prod_re = cr[:, :, :, None, :] * bbr_t[:, :, None, :, :] - ci[:, :, :, None, :] * bbi_t[:, :, None, :, :]
    prod_im = cr[:, :, :, None, :] * bbi_t[:, :, None, :, :] + ci[:, :, :, None, :] * bbr_t[:, :, None, :, :]
    kern = (jnp.einsum('kdgp,dgoip->kdgoi', pwr, prod_re, precision=hp)
            - jnp.einsum('kdgp,dgoip->kdgoi', pwi, prod_im, precision=hp))
    kf, kb = kern[:, 0], kern[:, 1]
    kall = jnp.concatenate([kb[:0:-1], (kf[0] + kb[0])[None], kf[1:], jnp.zeros_like(kf[:1])], axis=0)
    m = _s5_toeplitz(jnp.transpose(kall, (1, 3, 2, 0)))

    bbr_p, bbi_p, cr_p, ci_p = lane_pad(bbr_t), lane_pad(bbi_t), lane_pad(cr), lane_pad(ci)

    def state_maps(exponents, d):
        pr, pi = power_per_step(exponents)
        pr, pi = pr[d][:, None], pi[d][:, None]
        br, bi = bbr_p[d][:, :, None, :], bbi_p[d][:, :, None, :]
        rows = lambda v: v.reshape(g, gc * t, 2 * p).astype(BF16)
        return rows(pr * br - pi * bi), rows(pr * bi + pi * br)

    f_slots = state_maps(t - 1.0 - steps, 0) + state_maps(steps, 1)

    def out_maps(exponents, d):
        pr, pi = power_per_step(exponents)
        pr = jnp.swapaxes(pr[d], -1, -2)[:, None]
        pi = jnp.swapaxes(pi[d], -1, -2)[:, None]
        c_r, c_i = cr_p[d][..., None], ci_p[d][..., None]
        cols = lambda v: v.astype(BF16)
        return cols(c_r * pr - c_i * pi), cols(-(c_r * pi + c_i * pr))

    e_slots = out_maps(steps + 1.0, 0) + out_maps(t - steps, 1)
    pad = lambda v, axis: jnp.concatenate([v, jnp.zeros_like(v)], axis=axis)

    step_r, step_i = power(jnp.array([1.0, 2.0, 4.0, 8.0], F32)[:, None, None, None] * t)
    rows_a = []
    for d in range(2):
        for s in range(4):
            rows_a += [step_r[s, d], step_i[s, d]]
    r8 = jnp.arange(8, dtype=F32)[:, None, None, None] * t
    car_r, car_i = power(r8)
    rows = jnp.concatenate([jnp.stack(rows_a, axis=0), car_r[:, 0], car_i[:, 0],
                            car_r[::-1, 1], car_i[::-1, 1]], axis=0)
    tab = pad(jnp.transpose(rows, (1, 0, 2)), 2)
    return m, f_slots, e_slots, tab


def _s5_kernel(u_ref, m_ref, f0_ref, f1_ref, f2_ref, f3_ref, e0_ref, e1_ref, e2_ref, e3_ref,
               tab_ref, y_ref, s_ref, xin_ref):
    bsz, gc, n_c, t = u_ref.shape
    lanes = 2 * S5_STATE
    u = jnp.concatenate(
        [jnp.concatenate([u_ref[b, i] for i in range(gc)], axis=1) for b in range(bsz)], axis=0)
    f_all = jnp.concatenate([r[0] for r in (f0_ref, f1_ref, f2_ref, f3_ref)], axis=1)
    e_all = jnp.concatenate(
        [jnp.concatenate([r[0, o] for o in range(gc)], axis=1)
         for r in (e0_ref, e1_ref, e2_ref, e3_ref)], axis=0)
    s_ref[...] = _dot(u, f_all)

    row = lax.broadcasted_iota(jnp.int32, (8, lanes), 0)
    tab_a = tab_ref[0, 0:16, :]
    bc = lambda v: jnp.broadcast_to(v, (8, lanes))
    step_w = [[(bc(tab_a[8 * d + 2 * s:8 * d + 2 * s + 1]), bc(tab_a[8 * d + 2 * s + 1:8 * d + 2 * s + 2]))
               for s in range(4)] for d in range(2)]
    car = [(tab_ref[0, 16 + 16 * d:24 + 16 * d, :], tab_ref[0, 24 + 16 * d:32 + 16 * d, :])
           for d in range(2)]

    def shifted(v, sh, d):
        if d == 0:
            return jnp.where(row >= sh, pltpu.roll(v, sh, 0), 0.0)
        return jnp.where(row < 8 - sh, pltpu.roll(v, 8 - sh, 0), 0.0)

    def tile_scan(r0, hr, hi, d):
        c0 = 2 * lanes * d
        zr = s_ref[r0:r0 + 8, c0:c0 + lanes]
        zi = s_ref[r0:r0 + 8, c0 + lanes:c0 + 2 * lanes]
        for s, sh in enumerate((1, 2, 4)):
            wr, wi = step_w[d][s]
            sr, si = shifted(zr, sh, d), shifted(zi, sh, d)
            zr, zi = zr + wr * sr - wi * si, zi + wr * si + wi * sr
        pr, pi = car[d]
        xin_ref[r0:r0 + 8, c0:c0 + lanes] = pr * hr - pi * hi + shifted(zr, 1, d)
        xin_ref[r0:r0 + 8, c0 + lanes:c0 + 2 * lanes] = pr * hi + pi * hr + shifted(zi, 1, d)
        last = 0 if d else 7
        wr, wi = step_w[d][3]
        return (wr * hr - wi * hi + bc(zr[last:last + 1]), wr * hi + wi * hr + bc(zi[last:last + 1]))

    n_t = n_c // 8
    zero = jnp.zeros((8, lanes), F32)
    for b in range(bsz):
        fr, fi, br, bi = zero, zero, zero, zero
        for k in range(n_t):
            fr, fi = tile_scan(b * n_c + 8 * k, fr, fi, 0)
            br, bi = tile_scan(b * n_c + 8 * (n_t - 1 - k), br, bi, 1)

    y = _dot(u, m_ref[0]) + _dot(xin_ref[...].astype(BF16), e_all)
    for b in range(bsz):
        for o in range(gc):
            y_ref[b, o] = y[b * n_c:(b + 1) * n_c, o * t:(o + 1) * t].astype(BF16)


def _s5_scan(u_t, m, f_slots, e_slots, tab):
    bsz, width, seqlen = u_t.shape
    n_c = seqlen // S5_T
    gc = S5_GROUP_CH
    rows = bsz * n_c
    f_spec = pl.BlockSpec((1, gc * S5_T, 2 * S5_STATE), lambda q: (q, 0, 0))
    e_spec = pl.BlockSpec((1, gc, 2 * S5_STATE, S5_T), lambda q: (q, 0, 0, 0))
    y = pl.pallas_call(
        _s5_kernel,
        grid=(S5_GROUPS,),
        in_specs=[
            pl.BlockSpec((bsz, gc, n_c, S5_T), lambda q: (0, q, 0, 0)),
            pl.BlockSpec((1, gc * S5_T, gc * S5_T), lambda q: (q, 0, 0)),
            f_spec, f_spec, f_spec, f_spec, e_spec, e_spec, e_spec, e_spec,
            pl.BlockSpec((1, 48, 2 * S5_STATE), lambda q: (q, 0, 0)),
        ],
        out_specs=pl.BlockSpec((bsz, gc, n_c, S5_T), lambda q: (0, q, 0, 0)),
        out_shape=jax.ShapeDtypeStruct((bsz, width, n_c, S5_T), BF16),
        scratch_shapes=[pltpu.VMEM((rows, 8 * S5_STATE), F32), pltpu.VMEM((rows, 8 * S5_STATE), F32)],
        compiler_params=_params("parallel"),
    )(u_t.reshape(bsz, width, n_c, S5_T), m, *f_slots, *e_slots, tab)
    return y.reshape(bsz, width, seqlen)


def _gelu_tanh(v):
    return 0.5 * v * (1.0 + jnp.tanh(math.sqrt(2.0 / math.pi) * (v + 0.044715 * (v * v * v))))


def _outproj_kernel(yssd_ref, ys5_ref, u_ref, x_ref, g1_ref, sc2_ref, sh2_ref, d5_ref,
                    wglu_t_ref, bglu_ref, wout_ref, lng_ref, lnb_ref, x1_ref, h2_ref):
    y = ys5_ref[0].astype(F32) + u_ref[0].astype(F32) * d5_ref[...]
    y = _gelu_tanh(y)
    s5 = y * jax.nn.sigmoid(_dot(wglu_t_ref[...], y.astype(BF16)) + bglu_ref[...])
    mix = (_dot(yssd_ref[0], wout_ref[:SSD_WIDTH, :])
           + _dot_tn(s5.astype(BF16), wout_ref[SSD_WIDTH:, :]))
    x1 = _layer_norm(DEEPNORM_ALPHA * x_ref[0] + g1_ref[0] * mix, lng_ref[...], lnb_ref[...])
    x1_ref[0] = x1
    h2_ref[0] = (x1 * (1.0 + sc2_ref[0]) + sh2_ref[0]).astype(BF16)


def _outproj(yssd, ys5_t, u_t, x, g1, sc2, sh2, d5, wglu, bglu, wout, lng, lnb, tm):
    bsz, seqlen, _ = x.shape
    tok = lambda b, i: (b, i, 0)
    tok_t = lambda b, i: (b, 0, i)
    per_b = lambda b, i: (b, 0, 0)
    const = lambda b, i: (0, 0)
    row = lambda v: v.astype(F32).reshape(1, -1)
    col = lambda v: v.astype(F32).reshape(-1, 1)
    return pl.pallas_call(
        _outproj_kernel,
        grid=(bsz, seqlen // tm),
        in_specs=[
            pl.BlockSpec((1, tm, SSD_WIDTH), tok),
            pl.BlockSpec((1, S5_WIDTH, tm), tok_t),
            pl.BlockSpec((1, S5_WIDTH, tm), tok_t),
            pl.BlockSpec((1, tm, D_MODEL), tok),
            pl.BlockSpec((1, 1, D_MODEL), per_b),
            pl.BlockSpec((1, 1, D_MODEL), per_b),
            pl.BlockSpec((1, 1, D_MODEL), per_b),
            pl.BlockSpec((S5_WIDTH, 1), const),
            pl.BlockSpec((S5_WIDTH, S5_WIDTH), const),
            pl.BlockSpec((S5_WIDTH, 1), const),
            pl.BlockSpec((D_MODEL, D_MODEL), const),
            pl.BlockSpec((1, D_MODEL), const),
            pl.BlockSpec((1, D_MODEL), const),
        ],
        out_specs=[pl.BlockSpec((1, tm, D_MODEL), tok), pl.BlockSpec((1, tm, D_MODEL), tok)],
        out_shape=[jax.ShapeDtypeStruct((bsz, seqlen, D_MODEL), F32),
                   jax.ShapeDtypeStruct((bsz, seqlen, D_MODEL), BF16)],
        compiler_params=_params("parallel", "parallel"),
    )(yssd, ys5_t, u_t, x, g1, sc2, sh2, col(d5), wglu.T.astype(BF16), col(bglu), wout.astype(BF16),
      row(lng), row(lnb))


def _round_up(v, m):
    return (v + m - 1) // m * m


def _route_subtile(h2_s, wr_t, br, tri):
    s_len = h2_s.shape[0]
    logits = _dot_nt(wr_t, h2_s) + br
    mx = jnp.max(logits, axis=0, keepdims=True)
    ex = jnp.exp(logits - mx)
    probs = ex / jnp.sum(ex, axis=0, keepdims=True)
    row = lax.broadcasted_iota(jnp.int32, (N_EXPERTS, s_len), 0)
    n_grp = N_EXPERTS // EXPERTS_PER_GROUP
    gs = [jnp.max(probs[EXPERTS_PER_GROUP * g:EXPERTS_PER_GROUP * (g + 1)], axis=0, keepdims=True)
          for g in range(n_grp)]
    gmax = functools.reduce(jnp.maximum, gs)
    sel_grp = jnp.full((1, s_len), n_grp - 1, jnp.int32)
    for g in range(n_grp - 2, -1, -1):
        sel_grp = jnp.where(gs[g] == gmax, g, sel_grp)
    in_grp = (row // EXPERTS_PER_GROUP) == sel_grp
    masked = jnp.where(in_grp, probs, -1.0)
    m1 = jnp.max(masked, axis=0, keepdims=True)
    i1 = jnp.min(jnp.where(masked == m1, row, N_EXPERTS), axis=0, keepdims=True)
    is1 = row == i1
    masked2 = jnp.where(is1, -2.0, masked)
    m2 = jnp.max(masked2, axis=0, keepdims=True)
    i2 = jnp.min(jnp.where(masked2 == m2, row, N_EXPERTS), axis=0, keepdims=True)
    is2 = row == i2
    den = m1 + m2
    sel = jnp.where(is1, 1.0, jnp.where(is2, 1.0, 0.0))
    pos = _dot(sel.astype(BF16), tri)
    return is1, is2, m1 / den, m2 / den, sel, pos


def _copy_rows(src_ref, src0, dst_ref, dst0, n_chunks):
    def body(c, carry):
        so = pl.multiple_of(src0 + c * MOE_ALIGN, MOE_ALIGN)
        do = pl.multiple_of(dst0 + c * MOE_ALIGN, MOE_ALIGN)
        dst_ref[pl.ds(do, MOE_ALIGN), :] = src_ref[pl.ds(so, MOE_ALIGN), :]
        return carry
    lax.fori_loop(0, n_chunks, body, 0)


def _zero_rows(dst_ref, dst0, n_chunks):
    def body(c, carry):
        do = pl.multiple_of(dst0 + c * MOE_ALIGN, MOE_ALIGN)
        dst_ref[pl.ds(do, MOE_ALIGN), :] = jnp.zeros((MOE_ALIGN, dst_ref.shape[1]), dst_ref.dtype)
        return carry
    lax.fori_loop(0, n_chunks, body, 0)


def _moe_kernel(h2_ref, x1_ref, g2_ref, wr_t_ref, br_ref, tri_ref, wg_ref, wu_ref, wd_ref,
                lng_ref, lnb_ref, out_ref, xg_ref, cb_ref, route_ref, meta_ref):
    step = pl.program_id(2)
    tm = h2_ref.shape[1]
    s_len = x1_ref.shape[1]
    n_sub = tm // s_len
    cb_rows = cb_ref.shape[0]
    ne = N_EXPERTS
    seg_len, seg_cb, seg_xg, e_base, e_blk, sub_rows = 0, 64, 128, 192, 208, 224

    def onehot_rows(r0, n_rows, t1, t2, v1, v2):
        slot = (lax.broadcasted_iota(jnp.int32, (n_rows, s_len), 0) + r0).astype(F32)
        return jnp.where(slot == t1, v1, jnp.where(slot == t2, v2, 0.0))

    @pl.when(step == 0)
    def _dispatch():
        erow = lax.broadcasted_iota(jnp.int32, (ne, 1), 0)
        for s in range(n_sub):
            h2_s = h2_ref[0, s * s_len:(s + 1) * s_len, :]
            is1, is2, g1, g2, sel, pos = _route_subtile(h2_s, wr_t_ref[...], br_ref[...], tri_ref[...])
            base_vec = jnp.zeros((ne, 1), F32)
            run = jnp.int32(0)
            for e in range(ne):
                n16 = _round_up(jnp.sum(sel[e:e + 1, :]).astype(jnp.int32), MOE_ALIGN)
                meta_ref[seg_len + s * ne + e] = n16
                meta_ref[seg_cb + s * ne + e] = run
                base_vec = jnp.where(erow == e, run.astype(F32), base_vec)
                run = run + n16
            meta_ref[sub_rows + s] = run
            tgt = base_vec + pos
            route_ref[s, 0:1, :] = jnp.sum(jnp.where(is1, tgt, 0.0), axis=0, keepdims=True)
            route_ref[s, 1:2, :] = jnp.sum(jnp.where(is2, tgt, 0.0), axis=0, keepdims=True)
            route_ref[s, 2:3, :] = g1
            route_ref[s, 3:4, :] = g2
        run = jnp.int32(0)
        for e in range(ne):
            rows_e = jnp.int32(0)
            for s in range(n_sub):
                meta_ref[seg_xg + s * ne + e] = run + rows_e
                rows_e = rows_e + meta_ref[seg_len + s * ne + e]
            blocks = (rows_e + MOE_ROWS - 1) // MOE_ROWS
            meta_ref[e_base + e] = run
            meta_ref[e_blk + e] = blocks
            _zero_rows(xg_ref, run + rows_e, (blocks * MOE_ROWS - rows_e) // MOE_ALIGN)
            run = run + blocks * MOE_ROWS
        for s in range(n_sub):
            h2_s = h2_ref[0, s * s_len:(s + 1) * s_len, :]
            t1, t2 = route_ref[s, 0:1, :], route_ref[s, 1:2, :]
            for r0 in range(0, cb_rows, MOE_GATHER_ROWS):
                onehot = onehot_rows(r0, MOE_GATHER_ROWS, t1, t2, 1.0, 1.0).astype(BF16)
                cb_ref[r0:r0 + MOE_GATHER_ROWS, :] = _dot(onehot, h2_s).astype(BF16)
            for e in range(ne):
                _copy_rows(cb_ref, meta_ref[seg_cb + s * ne + e], xg_ref, meta_ref[seg_xg + s * ne + e],
                           meta_ref[seg_len + s * ne + e] // MOE_ALIGN)

    n_expert_steps = ne // MOE_EXPERTS_PER_STEP

    def run_expert(k):
        e = step * MOE_EXPERTS_PER_STEP + k
        base = meta_ref[e_base + e]
        n_blk = meta_ref[e_blk + e]

        def run_rows(r0, rows):
            xe = xg_ref[pl.ds(r0, rows), :]
            hid = (_silu(_dot(xe, wg_ref[k])) * _dot(xe, wu_ref[k])).astype(BF16)
            xg_ref[pl.ds(r0, rows), :] = _dot(hid, wd_ref[k]).astype(BF16)

        def pair(j, carry):
            run_rows(pl.multiple_of(base + j * 2 * MOE_ROWS, MOE_ROWS), 2 * MOE_ROWS)
            return carry

        lax.fori_loop(0, n_blk // 2, pair, 0)

        @pl.when(n_blk % 2 == 1)
        def _tail():
            run_rows(pl.multiple_of(base + (n_blk - 1) * MOE_ROWS, MOE_ROWS), MOE_ROWS)

    @pl.when(step < n_expert_steps)
    def _experts():
        for k in range(MOE_EXPERTS_PER_STEP):
            run_expert(k)

    @pl.when(step >= n_expert_steps)
    def _combine():
        s = step - n_expert_steps
        for e in range(ne):
            _copy_rows(xg_ref, meta_ref[seg_xg + s * ne + e], cb_ref, meta_ref[seg_cb + s * ne + e],
                       meta_ref[seg_len + s * ne + e] // MOE_ALIGN)
        used = meta_ref[sub_rows + s]
        _zero_rows(cb_ref, used, (cb_rows - used) // MOE_ALIGN)
        t1, t2 = route_ref[s, 0:1, :], route_ref[s, 1:2, :]
        g1, g2 = route_ref[s, 2:3, :], route_ref[s, 3:4, :]
        acc = jnp.zeros((s_len, D_MODEL), F32)
        for r0 in range(0, cb_rows, MOE_GATHER_ROWS):
            weighted = onehot_rows(r0, MOE_GATHER_ROWS, t1, t2, g1, g2).astype(BF16)
            acc = acc + _dot_tn(weighted, cb_ref[r0:r0 + MOE_GATHER_ROWS, :])
        v = DEEPNORM_ALPHA * x1_ref[0] + g2_ref[0] * acc
        out_ref[0] = _layer_norm(v, lng_ref[...], lnb_ref[...])


def _moe(h2, x1, g2, wr_t, br, tri, wg, wu, wd, lng, lnb, tm, s_len):
    bsz, seqlen, _ = x1.shape
    n_sub = tm // s_len
    ne = N_EXPERTS
    cb_rows = _round_up(2 * s_len + ne * (MOE_ALIGN - 1), MOE_GATHER_ROWS)
    xg_rows = _round_up(2 * tm + n_sub * ne * (MOE_ALIGN - 1) + ne * (MOE_ROWS - 1), MOE_ROWS)
    tok = lambda b, i, st: (b, i, 0)
    n_es = ne // MOE_EXPERTS_PER_STEP
    sub = lambda b, i, st: (b, i * n_sub + jnp.clip(st - n_es, 0, n_sub - 1), 0)
    per_b = lambda b, i, st: (b, 0, 0)
    const = lambda b, i, st: (0, 0)
    per_e = lambda b, i, st: (jnp.minimum(st, n_es - 1), 0, 0)
    row = lambda v: v.astype(F32).reshape(1, -1)
    return pl.pallas_call(
        _moe_kernel,
        grid=(bsz, seqlen // tm, n_es + n_sub),
        in_specs=[
            pl.BlockSpec((1, tm, D_MODEL), tok),
            pl.BlockSpec((1, s_len, D_MODEL), sub),
            pl.BlockSpec((1, 1, D_MODEL), per_b),
            pl.BlockSpec((ne, D_MODEL), const),
            pl.BlockSpec((ne, 1), const),
            pl.BlockSpec((s_len, s_len), const),
            pl.BlockSpec((MOE_EXPERTS_PER_STEP, D_MODEL, D_EXPERT), per_e),
            pl.BlockSpec((MOE_EXPERTS_PER_STEP, D_MODEL, D_EXPERT), per_e),
            pl.BlockSpec((MOE_EXPERTS_PER_STEP, D_EXPERT, D_MODEL), per_e),
            pl.BlockSpec((1, D_MODEL), const),
            pl.BlockSpec((1, D_MODEL), const),
        ],
        out_specs=pl.BlockSpec((1, s_len, D_MODEL), sub),
        out_shape=jax.ShapeDtypeStruct((bsz, seqlen, D_MODEL), F32),
        scratch_shapes=[pltpu.VMEM((xg_rows, D_MODEL), BF16), pltpu.VMEM((cb_rows, D_MODEL), BF16),
                        pltpu.VMEM((n_sub, 8, s_len), F32), pltpu.SMEM((256,), jnp.int32)],
        compiler_params=_params("parallel", "parallel", "arbitrary"),
    )(h2, x1, g2, wr_t, br, tri, wg, wu, wd, row(lng), row(lnb))


def _layer_weights(i, w):
    w_in = w['w_in'][i]
    n_dt = 2 * SSD_HEADS
    dt_lo = SSD_WIDTH + SSD_CONV_CH
    w_dt = w_in[:, dt_lo:dt_lo + n_dt]
    w_u = w_in[:, dt_lo + n_dt:]
    w_all = jnp.concatenate(
        [w_in[:, :dt_lo], w_dt, jnp.zeros((D_MODEL, DT_PAD - n_dt), w_in.dtype)], axis=1).astype(BF16)
    w_t = jnp.concatenate([w_u, w_dt], axis=1).T.astype(BF16)
    m, f, e, tab = _s5_tables(w['s5_lam_re'][i], w['s5_lam_im'][i], w['s5_log_step'][i],
                              w['s5_b_re'][i], w['s5_b_im'][i], w['s5_c_re'][i], w['s5_c_im'][i])
    return dict(
        w_all=w_all, w_t=w_t,
        s5=(m, f, e, tab),
        wg=w['w_gate'][i].astype(BF16), wu=w['w_up'][i].astype(BF16), wd=w['w_down'][i].astype(BF16),
    )


def _trunk(x, mods, w, layer_w, tiles):
    bsz, seqlen, _ = x.shape
    tm, tq, tmoe = tiles
    wr_t = w['w_router'].T.astype(BF16)
    br = w['b_router'].astype(F32).reshape(N_EXPERTS, 1)
    s_len = min(MOE_SUB, tmoe)
    ids = jnp.arange(s_len)
    tri = (ids[:, None] < ids[None, :]).astype(BF16)
    for i in range(DEPTH):
        lw = layer_w[i]
        sh1, sc1, g1, sh2, sc2, g2 = [v.reshape(bsz, 1, D_MODEL) for v in jnp.split(mods[i], 6, axis=-1)]
        z, xbc, dt, u_t, dt_t = _inproj(x, sc1, sh1, lw['w_all'], lw['w_t'],
                                        w['conv_w'][i], w['conv_b'][i], tm)
        y_f = _ssd_dir(xbc, dt, dt_t, w['ssd_a_log'][i], w['ssd_dt_bias'][i], False, tq)
        y_ssd = _ssd_dir(xbc, dt, dt_t, w['ssd_a_log'][i], w['ssd_dt_bias'][i], True, tq,
                         z=z, yf=y_f, d_skip=w['ssd_d'][i], norm_w=w['ssd_norm_w'][i])
        y_s5_t = _s5_scan(u_t, *lw['s5'])
        x1, h2 = _outproj(y_ssd, y_s5_t, u_t, x, g1, sc2, sh2, w['s5_d'][i], w['s5_w_glu'][i],
                          w['s5_b_glu'][i], w['w_out'][i], w['ln1_g'][i], w['ln1_b'][i], tm)
        x = _moe(h2, x1, g2, wr_t, br, tri, lw['wg'], lw['wu'], lw['wd'],
                 w['ln2_g'][i], w['ln2_b'][i], tmoe, s_len)
    return x


def _pick(n, pref):
    return pref if n % pref == 0 else n


def _run(x_groups, c_groups, w):
    rows = sum(c.shape[0] for c in c_groups)
    pad = (-rows) % 8
    c_all = jnp.concatenate(list(c_groups) + [jnp.zeros((pad, D_MODEL), F32)], axis=0).astype(F32)
    mods = _ada_mod(c_all, w['w_ada'].astype(F32), w['b_ada'].astype(F32))
    layer_w = [_layer_weights(i, w) for i in range(DEPTH)]
    outs = []
    r0 = 0
    for x, c in zip(x_groups, c_groups):
        bsz, seqlen, _ = x.shape
        assert seqlen % (8 * S5_T) == 0, "sequence length must be a multiple of 8 S5 chunks"
        tiles = (_pick(seqlen, 1024), _pick(seqlen, 1024), _pick(seqlen, 2048))
        outs.append(_trunk(x, mods[:, r0:r0 + bsz], w, layer_w, tiles))
        r0 += bsz
    return tuple(outs)


def kernel(x_prompt, x_sample, c_prompt, c_sample, w_ada, b_ada, w_in, conv_w, conv_b, ssd_a_log, ssd_dt_bias, ssd_d, ssd_norm_w, s5_lam_re, s5_lam_im, s5_log_step, s5_b_re, s5_b_im, s5_c_re, s5_c_im, s5_d, s5_w_glu, s5_b_glu, w_out, ln1_g, ln1_b, ln2_g, ln2_b, w_router, b_router, w_gate, w_up, w_down):
    w = dict(w_ada=w_ada, b_ada=b_ada, w_in=w_in, conv_w=conv_w, conv_b=conv_b, ssd_a_log=ssd_a_log,
             ssd_dt_bias=ssd_dt_bias, ssd_d=ssd_d, ssd_norm_w=ssd_norm_w, s5_lam_re=s5_lam_re,
             s5_lam_im=s5_lam_im, s5_log_step=s5_log_step, s5_b_re=s5_b_re, s5_b_im=s5_b_im,
             s5_c_re=s5_c_re, s5_c_im=s5_c_im, s5_d=s5_d, s5_w_glu=s5_w_glu, s5_b_glu=s5_b_glu,
             w_out=w_out, ln1_g=ln1_g, ln1_b=ln1_b, ln2_g=ln2_g, ln2_b=ln2_b, w_router=w_router,
             b_router=b_router, w_gate=w_gate, w_up=w_up, w_down=w_down)
    return _run((x_prompt, x_sample), (c_prompt, c_sample), w)
```

```python
import functools
import math

import jax
import jax.numpy as jnp
from jax import lax
from jax.experimental import pallas as pl
from jax.experimental.pallas import tpu as pltpu

F32 = jnp.float32
BF16 = jnp.bfloat16

D_MODEL = 1024
DEPTH = 4
SSD_WIDTH = 512
SSD_HEADDIM = 64
SSD_HEADS = 8
SSD_GROUPS = 2
SSD_STATE = 128
SSD_CONV = 5
SSD_CONV_CH = SSD_WIDTH + 2 * SSD_GROUPS * SSD_STATE
S5_WIDTH = 512
S5_GROUP_CH = 16
S5_GROUPS = 32
S5_STATE = 64
S5_MIN_DECAY = 1e-4
N_EXPERTS = 16
EXPERTS_PER_GROUP = 4
D_EXPERT = 512
DEEPNORM_ALPHA = (2 * DEPTH) ** 0.25
LN_EPS = 1e-5
RMS_EPS = 1e-5

SSD_Q = 128
S5_T = 128
PROJ_COLS = SSD_WIDTH + SSD_CONV_CH
DT_PAD = 128
CONV_HALO = 16
MOE_ROWS = 128
MOE_ALIGN = 16
MOE_GATHER_ROWS = 256
MOE_SUB = 512
MOE_EXPERTS_PER_STEP = 2
VMEM_LIMIT = 56 * 1024 * 1024


def _dot(a, b):
    return jnp.dot(a, b, preferred_element_type=F32)


def _dot_nt(a, b):
    return lax.dot_general(a, b, (((1,), (1,)), ((), ())), preferred_element_type=F32)


def _dot_tn(a, b):
    return lax.dot_general(a, b, (((0,), (0,)), ((), ())), preferred_element_type=F32)


def _split_bf16(v):
    hi = v.astype(BF16)
    lo = (v - hi.astype(F32)).astype(BF16)
    return hi, lo


def _silu(v):
    return v * jax.nn.sigmoid(v)


def _layer_norm(v, g, b):
    mu = jnp.mean(v, axis=-1, keepdims=True)
    vc = v - mu
    var = jnp.mean(vc * vc, axis=-1, keepdims=True)
    return vc * lax.rsqrt(var + LN_EPS) * g + b


def _params(*sem):
    return pltpu.CompilerParams(dimension_semantics=sem, vmem_limit_bytes=VMEM_LIMIT)


def _ada_kernel(c_ref, w_ref, b_ref, o_ref):
    ca = _silu(c_ref[...])
    c_hi, c_lo = _split_bf16(ca)
    w_hi, w_lo = _split_bf16(w_ref[0])
    acc = _dot(c_hi, w_hi) + _dot(c_lo, w_hi) + _dot(c_hi, w_lo)
    o_ref[0] = acc + b_ref[0]


def _ada_mod(c_all, w_ada, b_ada):
    rows = c_all.shape[0]
    n_out = w_ada.shape[-1]
    bn = 1536
    return pl.pallas_call(
        _ada_kernel,
        grid=(DEPTH, n_out // bn),
        in_specs=[
            pl.BlockSpec((rows, D_MODEL), lambda l, j: (0, 0)),
            pl.BlockSpec((1, D_MODEL, bn), lambda l, j: (l, 0, j)),
            pl.BlockSpec((1, 1, bn), lambda l, j: (l, 0, j)),
        ],
        out_specs=pl.BlockSpec((1, rows, bn), lambda l, j: (l, 0, j)),
        out_shape=jax.ShapeDtypeStruct((DEPTH, rows, n_out), F32),
        compiler_params=_params("arbitrary", "arbitrary"),
    )(c_all, w_ada, b_ada.reshape(DEPTH, 1, n_out))


def _inproj_kernel(x_ref, xl_ref, xr_ref, sc_ref, sh_ref, w_ref, w_t_ref, cw_ref, cb_ref,
                   z_ref, xbc_ref, dt_ref, u_t_ref, dt_t_ref, ext_ref):
    i = pl.program_id(1)
    n = pl.num_programs(1)
    tm = x_ref.shape[1]
    scale = 1.0 + sc_ref[0]
    shift = sh_ref[0]
    h = (x_ref[0] * scale + shift).astype(BF16)
    p = _dot(h, w_ref[...])
    z_ref[0] = p[:, :SSD_WIDTH].astype(BF16)
    dt_ref[0] = p[:, PROJ_COLS:PROJ_COLS + 2 * SSD_HEADS]
    p_t = _dot_nt(w_t_ref[...], h)
    u_t_ref[0] = p_t[:S5_WIDTH].astype(BF16)
    dt_t_ref[0] = p_t[S5_WIDTH:]

    halo = jnp.concatenate([xl_ref[0], xr_ref[0]], axis=0)
    p_halo = _dot((halo * scale + shift).astype(BF16), w_ref[:, SSD_WIDTH:PROJ_COLS])
    has_left = (i > 0).astype(F32)
    has_right = (i < n - 1).astype(F32)
    ext_ref[0:CONV_HALO, :] = p_halo[:CONV_HALO] * has_left
    ext_ref[CONV_HALO:CONV_HALO + tm, :] = p[:, SSD_WIDTH:PROJ_COLS]
    ext_ref[CONV_HALO + tm:2 * CONV_HALO + tm, :] = p_halo[CONV_HALO:] * has_right
    first = CONV_HALO - SSD_CONV // 2
    acc = cb_ref[...] + ext_ref[first:first + tm, :] * cw_ref[0:1, :]
    for k in range(1, SSD_CONV):
        acc = acc + ext_ref[first + k:first + k + tm, :] * cw_ref[k:k + 1, :]
    xbc_ref[0] = _silu(acc).astype(BF16)


def _inproj(x, sc, sh, w_all, w_t, conv_w, conv_b, tm):
    bsz, seqlen, _ = x.shape
    per_blk = tm // CONV_HALO
    n_halo = seqlen // CONV_HALO
    tok = lambda b, i: (b, i, 0)
    tok_t = lambda b, i: (b, 0, i)
    per_b = lambda b, i: (b, 0, 0)
    const = lambda b, i: (0, 0)
    w_pad = jnp.zeros((8, SSD_CONV_CH), F32).at[:SSD_CONV].set(conv_w.astype(F32))
    return pl.pallas_call(
        _inproj_kernel,
        grid=(bsz, seqlen // tm),
        in_specs=[
            pl.BlockSpec((1, tm, D_MODEL), tok),
            pl.BlockSpec((1, CONV_HALO, D_MODEL), lambda b, i: (b, jnp.maximum(i * per_blk - 1, 0), 0)),
            pl.BlockSpec((1, CONV_HALO, D_MODEL),
                         lambda b, i: (b, jnp.minimum((i + 1) * per_blk, n_halo - 1), 0)),
            pl.BlockSpec((1, 1, D_MODEL), per_b),
            pl.BlockSpec((1, 1, D_MODEL), per_b),
            pl.BlockSpec(w_all.shape, const),
            pl.BlockSpec(w_t.shape, const),
            pl.BlockSpec((8, SSD_CONV_CH), const),
            pl.BlockSpec((1, SSD_CONV_CH), const),
        ],
        out_specs=[
            pl.BlockSpec((1, tm, SSD_WIDTH), tok),
            pl.BlockSpec((1, tm, SSD_CONV_CH), tok),
            pl.BlockSpec((1, tm, 2 * SSD_HEADS), tok),
            pl.BlockSpec((1, S5_WIDTH, tm), tok_t),
            pl.BlockSpec((1, 2 * SSD_HEADS, tm), tok_t),
        ],
        out_shape=[
            jax.ShapeDtypeStruct((bsz, seqlen, SSD_WIDTH), BF16),
            jax.ShapeDtypeStruct((bsz, seqlen, SSD_CONV_CH), BF16),
            jax.ShapeDtypeStruct((bsz, seqlen, 2 * SSD_HEADS), F32),
            jax.ShapeDtypeStruct((bsz, S5_WIDTH, seqlen), BF16),
            jax.ShapeDtypeStruct((bsz, 2 * SSD_HEADS, seqlen), F32),
        ],
        scratch_shapes=[pltpu.VMEM((tm + 2 * CONV_HALO, SSD_CONV_CH), F32)],
        compiler_params=_params("parallel", "parallel"),
    )(x, x, x, sc, sh, w_all, w_t, w_pad, conv_b.astype(F32).reshape(1, SSD_CONV_CH))


def _softplus(v):
    return jnp.maximum(v, 0.0) + jnp.log1p(jnp.exp(-jnp.abs(v)))


def _head_indicator(width):
    head = lax.broadcasted_iota(jnp.int32, (SSD_HEADS, SSD_HEADS * width), 0)
    lane = lax.broadcasted_iota(jnp.int32, (SSD_HEADS, SSD_HEADS * width), 1)
    return jnp.where(lane // width == head, 1.0, 0.0).astype(BF16)


def _ssd_chunk(xbc, dt_raw, dt_raw_t, a_row, a_col, bias_row, bias_col, state_ref, rev):
    q = xbc.shape[0]
    x = xbc[:, :SSD_WIDTH]
    b_mat = xbc[:, SSD_WIDTH:SSD_WIDTH + SSD_GROUPS * SSD_STATE]
    c_mat = xbc[:, SSD_WIDTH + SSD_GROUPS * SSD_STATE:]
    dts = _softplus(dt_raw + bias_row)
    dts_t = _softplus(dt_raw_t + bias_col)
    dta = dts * a_row
    dta_t = dts_t * a_col

    row = lax.broadcasted_iota(jnp.int32, (q, q), 0)
    col = lax.broadcasted_iota(jnp.int32, (q, q), 1)
    lower = col <= row
    upper = col >= row
    tri_lower = jnp.where(lower, 1.0, 0.0).astype(BF16)
    tri_upper = jnp.where(upper, 1.0, 0.0).astype(BF16)
    tri_c, tri_r, mask = (tri_upper, tri_lower, upper) if rev else (tri_lower, tri_upper, lower)

    d_hi, d_lo = _split_bf16(dta)
    acc = _dot(tri_c, d_hi) + _dot(tri_c, d_lo)
    t_hi, t_lo = _split_bf16(dta_t)
    acc_t = _dot(t_hi, tri_r) + _dot(t_lo, tri_r)

    lane128 = lax.broadcasted_iota(jnp.int32, (q, 128), 1)
    x32 = x.astype(F32)
    cb = [_dot_nt(c_mat[:, g * SSD_STATE:(g + 1) * SSD_STATE],
                  b_mat[:, g * SSD_STATE:(g + 1) * SSD_STATE]) for g in range(SSD_GROUPS)]
    a_hi, a_lo = _split_bf16(acc)
    ind_q = _head_indicator(q)
    t_row = jnp.concatenate([acc_t[h:h + 1, :] for h in range(SSD_HEADS)], axis=1)
    seg_all = _dot(a_hi, ind_q) + _dot(a_lo, ind_q) - t_row
    ys = []
    for k in range(SSD_HEADS // 2):
        m_pair = []
        for hh in range(2):
            h = 2 * k + hh
            g = h // (SSD_HEADS // SSD_GROUPS)
            seg = seg_all[:, h * q:(h + 1) * q]
            dec = jnp.where(mask, jnp.exp(jnp.minimum(seg, 0.0)), 0.0)
            m_pair.append((cb[g] * dec * dts_t[h:h + 1, :]).astype(BF16))
        xp = x32[:, 128 * k:128 * (k + 1)]
        x_blk = jnp.concatenate([jnp.where(lane128 < SSD_HEADDIM, xp, 0.0).astype(BF16),
                                 jnp.where(lane128 < SSD_HEADDIM, 0.0, xp).astype(BF16)], axis=0)
        ys.append(_dot(jnp.concatenate(m_pair, axis=1), x_blk))
    y = jnp.concatenate(ys, axis=1)

    end = acc[0:1, :] if rev else acc[q - 1:q, :]
    ind_p = _head_indicator(SSD_HEADDIM)
    y_scale = _dot(jnp.exp(acc).astype(BF16), ind_p)
    x_w = _dot((dts * jnp.exp(end - acc)).astype(BF16), ind_p)
    c_hi, c_lo = _split_bf16(jnp.broadcast_to(jnp.exp(end), (8, SSD_HEADS)))
    carry = (_dot(c_hi, ind_p) + _dot(c_lo, ind_p))[0:1, :]
    xw = (x32 * x_w).astype(BF16)
    half = SSD_WIDTH // SSD_GROUPS
    y_off = []
    for g in range(SSD_GROUPS):
        st = state_ref[g]
        y_off.append(_dot(c_mat[:, g * SSD_STATE:(g + 1) * SSD_STATE], st.astype(BF16)))
        state_ref[g] = st * carry[:, g * half:(g + 1) * half] + _dot_tn(
            b_mat[:, g * SSD_STATE:(g + 1) * SSD_STATE], xw[:, g * half:(g + 1) * half])
    return y + jnp.concatenate(y_off, axis=1) * y_scale


def _ssd_dir_kernel(*refs, rev):
    if rev:
        (xbc_ref, dt_ref, dt_t_ref, alog_ref, alog_t_ref, bias_ref, bias_t_ref,
         z_ref, yf_ref, dskip_ref, nw_ref, o_ref, state_ref) = refs
    else:
        (xbc_ref, dt_ref, dt_t_ref, alog_ref, alog_t_ref, bias_ref, bias_t_ref,
         o_ref, state_ref) = refs

    @pl.when(pl.program_id(1) == 0)
    def _():
        state_ref[...] = jnp.zeros(state_ref.shape, F32)

    d = 1 if rev else 0
    hs = slice(d * SSD_HEADS, (d + 1) * SSD_HEADS)
    a_row = -jnp.exp(alog_ref[:, hs])
    a_col = -jnp.exp(alog_t_ref[hs, :])
    bias_row = bias_ref[:, hs]
    bias_col = bias_t_ref[hs, :]
    tq = xbc_ref.shape[1]
    n_chunks = tq // SSD_Q
    order = range(n_chunks - 1, -1, -1) if rev else range(n_chunks)
    for ci in order:
        rs = slice(ci * SSD_Q, (ci + 1) * SSD_Q)
        xbc = xbc_ref[0, rs, :]
        y = _ssd_chunk(xbc, dt_ref[0, rs, hs], dt_t_ref[0, hs, rs], a_row, a_col,
                       bias_row, bias_col, state_ref, rev)
        if rev:
            y = y + yf_ref[0, rs, :].astype(F32) + xbc[:, :SSD_WIDTH].astype(F32) * dskip_ref[...]
            y = y * _silu(z_ref[0, rs, :].astype(F32))
            y = y * lax.rsqrt(jnp.mean(y * y, axis=-1, keepdims=True) + RMS_EPS) * nw_ref[...]
            o_ref[0, rs, :] = y.astype(BF16)
        else:
            o_ref[0, rs, :] = y.astype(BF16)


def _ssd_dir(xbc, dt, dt_t, a_log, dt_bias, rev, tq, z=None, yf=None, d_skip=None, norm_w=None):
    bsz, seqlen, _ = xbc.shape
    nblk = seqlen // tq
    if rev:
        tok = lambda b, i: (b, nblk - 1 - i, 0)
        tok_t = lambda b, i: (b, 0, nblk - 1 - i)
    else:
        tok = lambda b, i: (b, i, 0)
        tok_t = lambda b, i: (b, 0, i)
    const = lambda b, i: (0, 0)
    nh2 = 2 * SSD_HEADS
    a_flat = a_log.astype(F32).reshape(1, nh2)
    bias_flat = dt_bias.astype(F32).reshape(1, nh2)
    args = [xbc, dt, dt_t, a_flat, a_flat.reshape(nh2, 1), bias_flat, bias_flat.reshape(nh2, 1)]
    in_specs = [
        pl.BlockSpec((1, tq, SSD_CONV_CH), tok),
        pl.BlockSpec((1, tq, nh2), tok),
        pl.BlockSpec((1, nh2, tq), tok_t),
        pl.BlockSpec((1, nh2), const),
        pl.BlockSpec((nh2, 1), const),
        pl.BlockSpec((1, nh2), const),
        pl.BlockSpec((nh2, 1), const),
    ]
    if rev:
        args += [z, yf, jnp.repeat(d_skip.astype(F32), SSD_HEADDIM).reshape(1, SSD_WIDTH),
                 norm_w.astype(F32).reshape(1, SSD_WIDTH)]
        in_specs += [
            pl.BlockSpec((1, tq, SSD_WIDTH), tok),
            pl.BlockSpec((1, tq, SSD_WIDTH), tok),
            pl.BlockSpec((1, SSD_WIDTH), const),
            pl.BlockSpec((1, SSD_WIDTH), const),
        ]
    return pl.pallas_call(
        functools.partial(_ssd_dir_kernel, rev=rev),
        grid=(bsz, nblk),
        in_specs=in_specs,
        out_specs=pl.BlockSpec((1, tq, SSD_WIDTH), tok),
        out_shape=jax.ShapeDtypeStruct((bsz, seqlen, SSD_WIDTH), BF16),
        scratch_shapes=[pltpu.VMEM((SSD_GROUPS, SSD_STATE, SSD_WIDTH // SSD_GROUPS), F32)],
        compiler_params=_params("parallel", "arbitrary"),
    )(*args)


def _s5_toeplitz_kernel(k_ref, m_ref):
    t = S5_T
    gc = S5_GROUP_CH
    def rows_of_channel(i, carry):
        r0 = pl.multiple_of(i * t, t)
        for o in range(gc):
            taps = jnp.broadcast_to(k_ref[0, i, o:o + 1, :], (t, 2 * t))
            skew = pltpu.roll(taps, t + 1, 1, stride=1, stride_axis=0)
            m_ref[0, pl.ds(r0, t), o * t:(o + 1) * t] = skew[:, :t].astype(BF16)
        return carry

    lax.fori_loop(0, gc, rows_of_channel, 0)


def _s5_toeplitz(kall):
    g, gc, _, taps = kall.shape
    n = gc * S5_T
    return pl.pallas_call(
        _s5_toeplitz_kernel,
        grid=(g,),
        in_specs=[pl.BlockSpec((1, gc, gc, taps), lambda q: (q, 0, 0, 0))],
        out_specs=pl.BlockSpec((1, n, n), lambda q: (q, 0, 0)),
        out_shape=jax.ShapeDtypeStruct((g, n, n), BF16),
        compiler_params=_params("parallel"),
    )(kall)


def _s5_tables(lam_re, lam_im, log_step, b_re, b_im, c_re, c_im):
    hp = lax.Precision.HIGHEST
    t = S5_T
    g, p, gc = S5_GROUPS, S5_STATE, S5_GROUP_CH
    lr = jnp.minimum(lam_re.astype(F32), -S5_MIN_DECAY)
    li = lam_im.astype(F32)
    step = jnp.exp(log_step.astype(F32))[..., None]
    ar, ai = lr * step, li * step

    def power(n):
        mag = jnp.exp(n * ar)
        return mag * jnp.cos(n * ai), mag * jnp.sin(n * ai)

    def lane_pad(v):
        return jnp.pad(v, [(0, 0)] * (v.ndim - 1) + [(0, p)])

    def power_per_step(n):
        arg_r = n[:, None] * lane_pad(ar)[:, :, None, :]
        arg_i = n[:, None] * lane_pad(ai)[:, :, None, :]
        mag = jnp.exp(arg_r)
        return mag * jnp.cos(arg_i), mag * jnp.sin(arg_i)

    lbr, lbi = power(1.0)
    den = lr * lr + li * li
    qr = ((lbr - 1.0) * lr + lbi * li) / den
    qi = (lbi * lr - (lbr - 1.0) * li) / den
    bre, bim = b_re.astype(F32)[None], b_im.astype(F32)[None]
    bbr = qr[..., None] * bre - qi[..., None] * bim
    bbi = qr[..., None] * bim + qi[..., None] * bre
    cr, ci = c_re.astype(F32), c_im.astype(F32)

    steps = jnp.arange(t, dtype=F32)
    pwr, pwi = power(steps[:, None, None, None])
    bbr_t, bbi_t = jnp.swapaxes(bbr, -1, -2), jnp.swapaxes(bbi, -1, -2)
    prod_re = cr[:, :, :, None, :] * bbr_t[:, :, None, :, :] - ci[:, :, :, None, :] * bbi_t[:, :, None, :, :]
    prod_im = cr[:, :, :, None, :] * bbi_t[:, :, None, :, :] + ci[:, :, :, None, :] * bbr_t[:, :, None, :, :]
    kern = (jnp.einsum('kdgp,dgoip->kdgoi', pwr, prod_re, precision=hp)
            - jnp.einsum('kdgp,dgoip->kdgoi', pwi, prod_im, precision=hp))
    kf, kb = kern[:, 0], kern[:, 1]
    kall = jnp.concatenate([kb[:0:-1], (kf[0] + kb[0])[None], kf[1:], jnp.zeros_like(kf[:1])], axis=0)
    m = _s5_toeplitz(jnp.transpose(kall, (1, 3, 2, 0)))

    bbr_p, bbi_p, cr_p, ci_p = lane_pad(bbr_t), lane_pad(bbi_t), lane_pad(cr), lane_pad(ci)

    def state_maps(exponents, d):
        pr, pi = power_per_step(exponents)
        pr, pi = pr[d][:, None], pi[d][:, None]
        br, bi = bbr_p[d][:, :, None, :], bbi_p[d][:, :, None, :]
        rows = lambda v: v.reshape(g, gc * t, 2 * p).astype(BF16)
        return rows(pr * br - pi * bi), rows(pr * bi + pi * br)

    f_slots = state_maps(t - 1.0 - steps, 0) + state_maps(steps, 1)

    def out_maps(exponents, d):
        pr, pi = power_per_step(exponents)
        pr = jnp.swapaxes(pr[d], -1, -2)[:, None]
        pi = jnp.swapaxes(pi[d], -1, -2)[:, None]
        c_r, c_i = cr_p[d][..., None], ci_p[d][..., None]
        cols = lambda v: v.astype(BF16)
        return cols(c_r * pr - c_i * pi), cols(-(c_r * pi + c_i * pr))

    e_slots = out_maps(steps + 1.0, 0) + out_maps(t - steps, 1)
    pad = lambda v, axis: jnp.concatenate([v, jnp.zeros_like(v)], axis=axis)

    step_r, step_i = power(jnp.array([1.0, 2.0, 4.0, 8.0], F32)[:, None, None, None] * t)
    rows_a = []
    for d in range(2):
        for s in range(4):
            rows_a += [step_r[s, d], step_i[s, d]]
    r8 = jnp.arange(8, dtype=F32)[:, None, None, None] * t
    car_r, car_i = power(r8)
    rows = jnp.concatenate([jnp.stack(rows_a, axis=0), car_r[:, 0], car_i[:, 0],
                            car_r[::-1, 1], car_i[::-1, 1]], axis=0)
    tab = pad(jnp.transpose(rows, (1, 0, 2)), 2)
    return m, f_slots, e_slots, tab


def _s5_kernel(u_ref, m_ref, f0_ref, f1_ref, f2_ref, f3_ref, e0_ref, e1_ref, e2_ref, e3_ref,
               tab_ref, y_ref, s_ref, xin_ref):
    bsz, gc, n_c, t = u_ref.shape
    lanes = 2 * S5_STATE
    u = jnp.concatenate(
        [jnp.concatenate([u_ref[b, i] for i in range(gc)], axis=1) for b in range(bsz)], axis=0)
    f_all = jnp.concatenate([r[0] for r in (f0_ref, f1_ref, f2_ref, f3_ref)], axis=1)
    e_all = jnp.concatenate(
        [jnp.concatenate([r[0, o] for o in range(gc)], axis=1)
         for r in (e0_ref, e1_ref, e2_ref, e3_ref)], axis=0)
    s_ref[...] = _dot(u, f_all)

    row = lax.broadcasted_iota(jnp.int32, (8, lanes), 0)
    tab_a = tab_ref[0, 0:16, :]
    bc = lambda v: jnp.broadcast_to(v, (8, lanes))
    step_w = [[(bc(tab_a[8 * d + 2 * s:8 * d + 2 * s + 1]), bc(tab_a[8 * d + 2 * s + 1:8 * d + 2 * s + 2]))
               for s in range(4)] for d in range(2)]
    car = [(tab_ref[0, 16 + 16 * d:24 + 16 * d, :], tab_ref[0, 24 + 16 * d:32 + 16 * d, :])
           for d in range(2)]

    def shifted(v, sh, d):
        if d == 0:
            return jnp.where(row >= sh, pltpu.roll(v, sh, 0), 0.0)
        return jnp.where(row < 8 - sh, pltpu.roll(v, 8 - sh, 0), 0.0)

    def tile_scan(r0, hr, hi, d):
        c0 = 2 * lanes * d
        zr = s_ref[r0:r0 + 8, c0:c0 + lanes]
        zi = s_ref[r0:r0 + 8, c0 + lanes:c0 + 2 * lanes]
        for s, sh in enumerate((1, 2, 4)):
            wr, wi = step_w[d][s]
            sr, si = shifted(zr, sh, d), shifted(zi, sh, d)
            zr, zi = zr + wr * sr - wi * si, zi + wr * si + wi * sr
        pr, pi = car[d]
        xin_ref[r0:r0 + 8, c0:c0 + lanes] = pr * hr - pi * hi + shifted(zr, 1, d)
        xin_ref[r0:r0 + 8, c0 + lanes:c0 + 2 * lanes] = pr * hi + pi * hr + shifted(zi, 1, d)
        last = 0 if d else 7
        wr, wi = step_w[d][3]
        return (wr * hr - wi * hi + bc(zr[last:last + 1]), wr * hi + wi * hr + bc(zi[last:last + 1]))

    n_t = n_c // 8
    zero = jnp.zeros((8, lanes), F32)
    for b in range(bsz):
        fr, fi, br, bi = zero, zero, zero, zero
        for k in range(n_t):
            fr, fi = tile_scan(b * n_c + 8 * k, fr, fi, 0)
            br, bi = tile_scan(b * n_c + 8 * (n_t - 1 - k), br, bi, 1)

    y = _dot(u, m_ref[0]) + _dot(xin_ref[...].astype(BF16), e_all)
    for b in range(bsz):
        for o in range(gc):
            y_ref[b, o] = y[b * n_c:(b + 1) * n_c, o * t:(o + 1) * t].astype(BF16)


def _s5_scan(u_t, m, f_slots, e_slots, tab):
    bsz, width, seqlen = u_t.shape
    n_c = seqlen // S5_T
    gc = S5_GROUP_CH
    rows = bsz * n_c
    f_spec = pl.BlockSpec((1, gc * S5_T, 2 * S5_STATE), lambda q: (q, 0, 0))
    e_spec = pl.BlockSpec((1, gc, 2 * S5_STATE, S5_T), lambda q: (q, 0, 0, 0))
    y = pl.pallas_call(
        _s5_kernel,
        grid=(S5_GROUPS,),
        in_specs=[
            pl.BlockSpec((bsz, gc, n_c, S5_T), lambda q: (0, q, 0, 0)),
            pl.BlockSpec((1, gc * S5_T, gc * S5_T), lambda q: (q, 0, 0)),
            f_spec, f_spec, f_spec, f_spec, e_spec, e_spec, e_spec, e_spec,
            pl.BlockSpec((1, 48, 2 * S5_STATE), lambda q: (q, 0, 0)),
        ],
        out_specs=pl.BlockSpec((bsz, gc, n_c, S5_T), lambda q: (0, q, 0, 0)),
        out_shape=jax.ShapeDtypeStruct((bsz, width, n_c, S5_T), BF16),
        scratch_shapes=[pltpu.VMEM((rows, 8 * S5_STATE), F32), pltpu.VMEM((rows, 8 * S5_STATE), F32)],
        compiler_params=_params("parallel"),
    )(u_t.reshape(bsz, width, n_c, S5_T), m, *f_slots, *e_slots, tab)
    return y.reshape(bsz, width, seqlen)


def _gelu_tanh(v):
    return 0.5 * v * (1.0 + jnp.tanh(math.sqrt(2.0 / math.pi) * (v + 0.044715 * (v * v * v))))


def _outproj_kernel(yssd_ref, ys5_ref, u_ref, x_ref, g1_ref, sc2_ref, sh2_ref, d5_ref,
                    wglu_t_ref, bglu_ref, wout_ref, lng_ref, lnb_ref, x1_ref, h2_ref):
    y = ys5_ref[0].astype(F32) + u_ref[0].astype(F32) * d5_ref[...]
    y = _gelu_tanh(y)
    s5 = y * jax.nn.sigmoid(_dot(wglu_t_ref[...], y.astype(BF16)) + bglu_ref[...])
    mix = (_dot(yssd_ref[0], wout_ref[:SSD_WIDTH, :])
           + _dot_tn(s5.astype(BF16), wout_ref[SSD_WIDTH:, :]))
    x1 = _layer_norm(DEEPNORM_ALPHA * x_ref[0] + g1_ref[0] * mix, lng_ref[...], lnb_ref[...])
    x1_ref[0] = x1
    h2_ref[0] = (x1 * (1.0 + sc2_ref[0]) + sh2_ref[0]).astype(BF16)


def _outproj(yssd, ys5_t, u_t, x, g1, sc2, sh2, d5, wglu, bglu, wout, lng, lnb, tm):
    bsz, seqlen, _ = x.shape
    tok = lambda b, i: (b, i, 0)
    tok_t = lambda b, i: (b, 0, i)
    per_b = lambda b, i: (b, 0, 0)
    const = lambda b, i: (0, 0)
    row = lambda v: v.astype(F32).reshape(1, -1)
    col = lambda v: v.astype(F32).reshape(-1, 1)
    return pl.pallas_call(
        _outproj_kernel,
        grid=(bsz, seqlen // tm),
        in_specs=[
            pl.BlockSpec((1, tm, SSD_WIDTH), tok),
            pl.BlockSpec((1, S5_WIDTH, tm), tok_t),
            pl.BlockSpec((1, S5_WIDTH, tm), tok_t),
            pl.BlockSpec((1, tm, D_MODEL), tok),
            pl.BlockSpec((1, 1, D_MODEL), per_b),
            pl.BlockSpec((1, 1, D_MODEL), per_b),
            pl.BlockSpec((1, 1, D_MODEL), per_b),
            pl.BlockSpec((S5_WIDTH, 1), const),
            pl.BlockSpec((S5_WIDTH, S5_WIDTH), const),
            pl.BlockSpec((S5_WIDTH, 1), const),
            pl.BlockSpec((D_MODEL, D_MODEL), const),
            pl.BlockSpec((1, D_MODEL), const),
            pl.BlockSpec((1, D_MODEL), const),
        ],
        out_specs=[pl.BlockSpec((1, tm, D_MODEL), tok), pl.BlockSpec((1, tm, D_MODEL), tok)],
        out_shape=[jax.ShapeDtypeStruct((bsz, seqlen, D_MODEL), F32),
                   jax.ShapeDtypeStruct((bsz, seqlen, D_MODEL), BF16)],
        compiler_params=_params("parallel", "parallel"),
    )(yssd, ys5_t, u_t, x, g1, sc2, sh2, col(d5), wglu.T.astype(BF16), col(bglu), wout.astype(BF16),
      row(lng), row(lnb))


def _round_up(v, m):
    return (v + m - 1) // m * m


def _route_subtile(h2_s, wr_t, br, tri):
    s_len = h2_s.shape[0]
    logits = _dot_nt(wr_t, h2_s) + br
    mx = jnp.max(logits, axis=0, keepdims=True)
    ex = jnp.exp(logits - mx)
    probs = ex / jnp.sum(ex, axis=0, keepdims=True)
    row = lax.broadcasted_iota(jnp.int32, (N_EXPERTS, s_len), 0)
    n_grp = N_EXPERTS // EXPERTS_PER_GROUP
    gs = [jnp.max(probs[EXPERTS_PER_GROUP * g:EXPERTS_PER_GROUP * (g + 1)], axis=0, keepdims=True)
          for g in range(n_grp)]
    gmax = functools.reduce(jnp.maximum, gs)
    sel_grp = jnp.full((1, s_len), n_grp - 1, jnp.int32)
    for g in range(n_grp - 2, -1, -1):
        sel_grp = jnp.where(gs[g] == gmax, g, sel_grp)
    in_grp = (row // EXPERTS_PER_GROUP) == sel_grp
    masked = jnp.where(in_grp, probs, -1.0)
    m1 = jnp.max(masked, axis=0, keepdims=True)
    i1 = jnp.min(jnp.where(masked == m1, row, N_EXPERTS), axis=0, keepdims=True)
    is1 = row == i1
    masked2 = jnp.where(is1, -2.0, masked)
    m2 = jnp.max(masked2, axis=0, keepdims=True)
    i2 = jnp.min(jnp.where(masked2 == m2, row, N_EXPERTS), axis=0, keepdims=True)
    is2 = row == i2
    den = m1 + m2
    sel = jnp.where(is1, 1.0, jnp.where(is2, 1.0, 0.0))
    pos = _dot(sel.astype(BF16), tri)
    return is1, is2, m1 / den, m2 / den, sel, pos


def _copy_rows(src_ref, src0, dst_ref, dst0, n_chunks):
    def body(c, carry):
        so = pl.multiple_of(src0 + c * MOE_ALIGN, MOE_ALIGN)
        do = pl.multiple_of(dst0 + c * MOE_ALIGN, MOE_ALIGN)
        dst_ref[pl.ds(do, MOE_ALIGN), :] = src_ref[pl.ds(so, MOE_ALIGN), :]
        return carry
    lax.fori_loop(0, n_chunks, body, 0)


def _zero_rows(dst_ref, dst0, n_chunks):
    def body(c, carry):
        do = pl.multiple_of(dst0 + c * MOE_ALIGN, MOE_ALIGN)
        dst_ref[pl.ds(do, MOE_ALIGN), :] = jnp.zeros((MOE_ALIGN, dst_ref.shape[1]), dst_ref.dtype)
        return carry
    lax.fori_loop(0, n_chunks, body, 0)


def _moe_kernel(h2_ref, x1_ref, g2_ref, wr_t_ref, br_ref, tri_ref, wg_ref, wu_ref, wd_ref,
                lng_ref, lnb_ref, out_ref, xg_ref, cb_ref, route_ref, meta_ref):
    step = pl.program_id(2)
    tm = h2_ref.shape[1]
    s_len = x1_ref.shape[1]
    n_sub = tm // s_len
    cb_rows = cb_ref.shape[0]
    ne = N_EXPERTS
    seg_len, seg_cb, seg_xg, e_base, e_blk, sub_rows = 0, 64, 128, 192, 208, 224

    def onehot_rows(r0, n_rows, t1, t2, v1, v2):
        slot = (lax.broadcasted_iota(jnp.int32, (n_rows, s_len), 0) + r0).astype(F32)
        return jnp.where(slot == t1, v1, jnp.where(slot == t2, v2, 0.0))

    @pl.when(step == 0)
    def _dispatch():
        erow = lax.broadcasted_iota(jnp.int32, (ne, 1), 0)
        for s in range(n_sub):
            h2_s = h2_ref[0, s * s_len:(s + 1) * s_len, :]
            is1, is2, g1, g2, sel, pos = _route_subtile(h2_s, wr_t_ref[...], br_ref[...], tri_ref[...])
            base_vec = jnp.zeros((ne, 1), F32)
            run = jnp.int32(0)
            for e in range(ne):
                n16 = _round_up(jnp.sum(sel[e:e + 1, :]).astype(jnp.int32), MOE_ALIGN)
                meta_ref[seg_len + s * ne + e] = n16
                meta_ref[seg_cb + s * ne + e] = run
                base_vec = jnp.where(erow == e, run.astype(F32), base_vec)
                run = run + n16
            meta_ref[sub_rows + s] = run
            tgt = base_vec + pos
            route_ref[s, 0:1, :] = jnp.sum(jnp.where(is1, tgt, 0.0), axis=0, keepdims=True)
            route_ref[s, 1:2, :] = jnp.sum(jnp.where(is2, tgt, 0.0), axis=0, keepdims=True)
            route_ref[s, 2:3, :] = g1
            route_ref[s, 3:4, :] = g2
        run = jnp.int32(0)
        for e in range(ne):
            rows_e = jnp.int32(0)
            for s in range(n_sub):
                meta_ref[seg_xg + s * ne + e] = run + rows_e
                rows_e = rows_e + meta_ref[seg_len + s * ne + e]
            blocks = (rows_e + MOE_ROWS - 1) // MOE_ROWS
            meta_ref[e_base + e] = run
            meta_ref[e_blk + e] = blocks
            _zero_rows(xg_ref, run + rows_e, (blocks * MOE_ROWS - rows_e) // MOE_ALIGN)
            run = run + blocks * MOE_ROWS
        for s in range(n_sub):
            h2_s = h2_ref[0, s * s_len:(s + 1) * s_len, :]
            t1, t2 = route_ref[s, 0:1, :], route_ref[s, 1:2, :]
            for r0 in range(0, cb_rows, MOE_GATHER_ROWS):
                onehot = onehot_rows(r0, MOE_GATHER_ROWS, t1, t2, 1.0, 1.0).astype(BF16)
                cb_ref[r0:r0 + MOE_GATHER_ROWS, :] = _dot(onehot, h2_s).astype(BF16)
            for e in range(ne):
                _copy_rows(cb_ref, meta_ref[seg_cb + s * ne + e], xg_ref, meta_ref[seg_xg + s * ne + e],
                           meta_ref[seg_len + s * ne + e] // MOE_ALIGN)

    n_expert_steps = ne // MOE_EXPERTS_PER_STEP

    def run_expert(k):
        e = step * MOE_EXPERTS_PER_STEP + k
        base = meta_ref[e_base + e]
        n_blk = meta_ref[e_blk + e]

        def run_rows(r0, rows):
            xe = xg_ref[pl.ds(r0, rows), :]
            hid = (_silu(_dot(xe, wg_ref[k])) * _dot(xe, wu_ref[k])).astype(BF16)
            xg_ref[pl.ds(r0, rows), :] = _dot(hid, wd_ref[k]).astype(BF16)

        def pair(j, carry):
            run_rows(pl.multiple_of(base + j * 2 * MOE_ROWS, MOE_ROWS), 2 * MOE_ROWS)
            return carry

        lax.fori_loop(0, n_blk // 2, pair, 0)

        @pl.when(n_blk % 2 == 1)
        def _tail():
            run_rows(pl.multiple_of(base + (n_blk - 1) * MOE_ROWS, MOE_ROWS), MOE_ROWS)

    @pl.when(step < n_expert_steps)
    def _experts():
        for k in range(MOE_EXPERTS_PER_STEP):
            run_expert(k)

    @pl.when(step >= n_expert_steps)
    def _combine():
        s = step - n_expert_steps
        for e in range(ne):
            _copy_rows(xg_ref, meta_ref[seg_xg + s * ne + e], cb_ref, meta_ref[seg_cb + s * ne + e],
                       meta_ref[seg_len + s * ne + e] // MOE_ALIGN)
        used = meta_ref[sub_rows + s]
        _zero_rows(cb_ref, used, (cb_rows - used) // MOE_ALIGN)
        t1, t2 = route_ref[s, 0:1, :], route_ref[s, 1:2, :]
        g1, g2 = route_ref[s, 2:3, :], route_ref[s, 3:4, :]
        acc = jnp.zeros((s_len, D_MODEL), F32)
        for r0 in range(0, cb_rows, MOE_GATHER_ROWS):
            weighted = onehot_rows(r0, MOE_GATHER_ROWS, t1, t2, g1, g2).astype(BF16)
            acc = acc + _dot_tn(weighted, cb_ref[r0:r0 + MOE_GATHER_ROWS, :])
        v = DEEPNORM_ALPHA * x1_ref[0] + g2_ref[0] * acc
        out_ref[0] = _layer_norm(v, lng_ref[...], lnb_ref[...])


def _moe(h2, x1, g2, wr_t, br, tri, wg, wu, wd, lng, lnb, tm, s_len):
    bsz, seqlen, _ = x1.shape
    n_sub = tm // s_len
    ne = N_EXPERTS
    cb_rows = _round_up(2 * s_len + ne * (MOE_ALIGN - 1), MOE_GATHER_ROWS)
    xg_rows = _round_up(2 * tm + n_sub * ne * (MOE_ALIGN - 1) + ne * (MOE_ROWS - 1), MOE_ROWS)
    tok = lambda b, i, st: (b, i, 0)
    n_es = ne // MOE_EXPERTS_PER_STEP
    sub = lambda b, i, st: (b, i * n_sub + jnp.clip(st - n_es, 0, n_sub - 1), 0)
    per_b = lambda b, i, st: (b, 0, 0)
    const = lambda b, i, st: (0, 0)
    per_e = lambda b, i, st: (jnp.minimum(st, n_es - 1), 0, 0)
    row = lambda v: v.astype(F32).reshape(1, -1)
    return pl.pallas_call(
        _moe_kernel,
        grid=(bsz, seqlen // tm, n_es + n_sub),
        in_specs=[
            pl.BlockSpec((1, tm, D_MODEL), tok),
            pl.BlockSpec((1, s_len, D_MODEL), sub),
            pl.BlockSpec((1, 1, D_MODEL), per_b),
            pl.BlockSpec((ne, D_MODEL), const),
            pl.BlockSpec((ne, 1), const),
            pl.BlockSpec((s_len, s_len), const),
            pl.BlockSpec((MOE_EXPERTS_PER_STEP, D_MODEL, D_EXPERT), per_e),
            pl.BlockSpec((MOE_EXPERTS_PER_STEP, D_MODEL, D_EXPERT), per_e),
            pl.BlockSpec((MOE_EXPERTS_PER_STEP, D_EXPERT, D_MODEL), per_e),
            pl.BlockSpec((1, D_MODEL), const),
            pl.BlockSpec((1, D_MODEL), const),
        ],
        out_specs=pl.BlockSpec((1, s_len, D_MODEL), sub),
        out_shape=jax.ShapeDtypeStruct((bsz, seqlen, D_MODEL), F32),
        scratch_shapes=[pltpu.VMEM((xg_rows, D_MODEL), BF16), pltpu.VMEM((cb_rows, D_MODEL), BF16),
                        pltpu.VMEM((n_sub, 8, s_len), F32), pltpu.SMEM((256,), jnp.int32)],
        compiler_params=_params("parallel", "parallel", "arbitrary"),
    )(h2, x1, g2, wr_t, br, tri, wg, wu, wd, row(lng), row(lnb))


def _layer_weights(i, w):
    w_in = w['w_in'][i]
    n_dt = 2 * SSD_HEADS
    dt_lo = SSD_WIDTH + SSD_CONV_CH
    w_dt = w_in[:, dt_lo:dt_lo + n_dt]
    w_u = w_in[:, dt_lo + n_dt:]
    w_all = jnp.concatenate(
        [w_in[:, :dt_lo], w_dt, jnp.zeros((D_MODEL, DT_PAD - n_dt), w_in.dtype)], axis=1).astype(BF16)
    w_t = jnp.concatenate([w_u, w_dt], axis=1).T.astype(BF16)
    m, f, e, tab = _s5_tables(w['s5_lam_re'][i], w['s5_lam_im'][i], w['s5_log_step'][i],
                              w['s5_b_re'][i], w['s5_b_im'][i], w['s5_c_re'][i], w['s5_c_im'][i])
    return dict(
        w_all=w_all, w_t=w_t,
        s5=(m, f, e, tab),
        wg=w['w_gate'][i].astype(BF16), wu=w['w_up'][i].astype(BF16), wd=w['w_down'][i].astype(BF16),
    )


def _trunk(x, mods, w, layer_w, tiles):
    bsz, seqlen, _ = x.shape
    tm, tq, tmoe = tiles
    wr_t = w['w_router'].T.astype(BF16)
    br = w['b_router'].astype(F32).reshape(N_EXPERTS, 1)
    s_len = min(MOE_SUB, tmoe)
    ids = jnp.arange(s_len)
    tri = (ids[:, None] < ids[None, :]).astype(BF16)
    for i in range(DEPTH):
        lw = layer_w[i]
        sh1, sc1, g1, sh2, sc2, g2 = [v.reshape(bsz, 1, D_MODEL) for v in jnp.split(mods[i], 6, axis=-1)]
        z, xbc, dt, u_t, dt_t = _inproj(x, sc1, sh1, lw['w_all'], lw['w_t'],
                                        w['conv_w'][i], w['conv_b'][i], tm)
        y_f = _ssd_dir(xbc, dt, dt_t, w['ssd_a_log'][i], w['ssd_dt_bias'][i], False, tq)
        y_ssd = _ssd_dir(xbc, dt, dt_t, w['ssd_a_log'][i], w['ssd_dt_bias'][i], True, tq,
                         z=z, yf=y_f, d_skip=w['ssd_d'][i], norm_w=w['ssd_norm_w'][i])
        y_s5_t = _s5_scan(u_t, *lw['s5'])
        x1, h2 = _outproj(y_ssd, y_s5_t, u_t, x, g1, sc2, sh2, w['s5_d'][i], w['s5_w_glu'][i],
                          w['s5_b_glu'][i], w['w_out'][i], w['ln1_g'][i], w['ln1_b'][i], tm)
        x = _moe(h2, x1, g2, wr_t, br, tri, lw['wg'], lw['wu'], lw['wd'],
                 w['ln2_g'][i], w['ln2_b'][i], tmoe, s_len)
    return x


def _pick(n, pref):
    return pref if n % pref == 0 else n


def _run(x_groups, c_groups, w):
    rows = sum(c.shape[0] for c in c_groups)
    pad = (-rows) % 8
    c_all = jnp.concatenate(list(c_groups) + [jnp.zeros((pad, D_MODEL), F32)], axis=0).astype(F32)
    mods = _ada_mod(c_all, w['w_ada'].astype(F32), w['b_ada'].astype(F32))
    layer_w = [_layer_weights(i, w) for i in range(DEPTH)]
    outs = []
    r0 = 0
    for x, c in zip(x_groups, c_groups):
        bsz, seqlen, _ = x.shape
        assert seqlen % (8 * S5_T) == 0, "sequence length must be a multiple of 8 S5 chunks"
        tiles = (_pick(seqlen, 1024), _pick(seqlen, 2048), _pick(seqlen, 2048))
        outs.append(_trunk(x, mods[:, r0:r0 + bsz], w, layer_w, tiles))
        r0 += bsz
    return tuple(outs)


def kernel(x_prompt, x_sample, c_prompt, c_sample, w_ada, b_ada, w_in, conv_w, conv_b, ssd_a_log, ssd_dt_bias, ssd_d, ssd_norm_w, s5_lam_re, s5_lam_im, s5_log_step, s5_b_re, s5_b_im, s5_c_re, s5_c_im, s5_d, s5_w_glu, s5_b_glu, w_out, ln1_g, ln1_b, ln2_g, ln2_b, w_router, b_router, w_gate, w_up, w_down):
    w = dict(w_ada=w_ada, b_ada=b_ada, w_in=w_in, conv_w=conv_w, conv_b=conv_b, ssd_a_log=ssd_a_log,
             ssd_dt_bias=ssd_dt_bias, ssd_d=ssd_d, ssd_norm_w=ssd_norm_w, s5_lam_re=s5_lam_re,
             s5_lam_im=s5_lam_im, s5_log_step=s5_log_step, s5_b_re=s5_b_re, s5_b_im=s5_b_im,
             s5_c_re=s5_c_re, s5_c_im=s5_c_im, s5_d=s5_d, s5_w_glu=s5_w_glu, s5_b_glu=s5_b_glu,
             w_out=w_out, ln1_g=ln1_g, ln1_b=ln1_b, ln2_g=ln2_g, ln2_b=ln2_b, w_router=w_router,
             b_router=b_router, w_gate=w_gate, w_up=w_up, w_down=w_down)
    return _run((x_prompt, x_sample), (c_prompt, c_sample), w)
```
